```python
import math
import jax, jax.numpy as jnp
from jax import lax
import numpy as np

D_MODEL = 2048
BATCH = 1
SEQ = 8192
DEPTH = 4

HEAD_DIM = 64
DIFF_HEADS = D_MODEL // 4 // (2 * HEAD_DIM)
FOX_HEADS = D_MODEL // 4 // HEAD_DIM
NSA_HEADS = D_MODEL // 2 // HEAD_DIM
NSA_GROUPS = 4
NSA_HPG = NSA_HEADS // NSA_GROUPS
CMP_BLOCK = 32
CMP_STRIDE = 16
CMP_HIDDEN = 256
SEL_BLOCK = 64
SEL_TOPK = 16
WINDOW = 512
QBLK = 128
NUM_BUCKETS = 32
MAX_DISTANCE = 128
N_GROUPS = 8
EXPERTS_PER_GROUP = 8
N_EXPERTS = N_GROUPS * EXPERTS_PER_GROUP
TOP_K_EXPERTS = 2
D_EXPERT = D_MODEL // 8
MOE_CHUNK = 128
RMS_EPS = 1e-6
NEG_INF = -1e30
SEL_FORCE = 1e4

SPLIT_SIZES = (
    DIFF_HEADS * 2 * HEAD_DIM,
    DIFF_HEADS * 2 * HEAD_DIM,
    DIFF_HEADS * 2 * HEAD_DIM,
    FOX_HEADS * HEAD_DIM,
    FOX_HEADS * HEAD_DIM,
    FOX_HEADS * HEAD_DIM,
    FOX_HEADS,
    NSA_HEADS * HEAD_DIM,
    NSA_GROUPS * HEAD_DIM,
    NSA_GROUPS * HEAD_DIM,
    NSA_GROUPS * HEAD_DIM,
    NSA_GROUPS * HEAD_DIM,
    NSA_GROUPS * HEAD_DIM,
    NSA_GROUPS * HEAD_DIM,
    3 * NSA_HEADS,
)
D_IN = sum(SPLIT_SIZES)
D_MIX = DIFF_HEADS * 2 * HEAD_DIM + FOX_HEADS * HEAD_DIM + NSA_HEADS * HEAD_DIM

kernel_name = "hybrid_diff_fox_nsa_hmoe"


def rms_norm(x, g):
    x32 = x.astype(jnp.float32)
    y = x32 * lax.rsqrt(jnp.mean(x32 * x32, axis=-1, keepdims=True) + RMS_EPS)
    return (y * g.astype(jnp.float32)).astype(x.dtype)


def masked_softmax(logits, mask):
    return jax.nn.softmax(jnp.where(mask, logits, NEG_INF), axis=-1)


def t5_bucket(rel):
    n = jnp.maximum(rel, 0)
    max_exact = NUM_BUCKETS // 2
    nf = jnp.maximum(n, max_exact).astype(jnp.float32)
    large = max_exact + (jnp.log(nf / max_exact) / math.log(MAX_DISTANCE / max_exact)
                         * (NUM_BUCKETS - max_exact)).astype(jnp.int32)
    return jnp.where(n < max_exact, n, jnp.minimum(large, NUM_BUCKETS - 1))


def diff_attention(q, k, v, lam, bias_table):
    B, S, H = q.shape[:3]
    scale = HEAD_DIM ** -0.5
    kpos = jnp.arange(S)
    k1, k2 = k[:, :, :, 0], k[:, :, :, 1]
    table = bias_table.astype(jnp.float32)

    def block(i):
        q0 = i * QBLK
        qb = lax.dynamic_slice_in_dim(q, q0, QBLK, axis=1)
        rel = (q0 + jnp.arange(QBLK))[:, None] - kpos[None, :]
        mask = rel >= 0
        bias = jnp.transpose(table[t5_bucket(rel)], (2, 0, 1))
        s1 = jnp.einsum('bqhd,bkhd->bhqk', qb[:, :, :, 0], k1).astype(jnp.float32) * scale + bias
        s2 = jnp.einsum('bqhd,bkhd->bhqk', qb[:, :, :, 1], k2).astype(jnp.float32) * scale + bias
        a = masked_softmax(s1, mask) - lam * masked_softmax(s2, mask)
        return jnp.einsum('bhqk,bkhd->bqhd', a.astype(v.dtype), v)

    out = lax.map(block, jnp.arange(S // QBLK))
    return jnp.moveaxis(out, 0, 1).reshape(B, S, H, v.shape[-1])


def forgetting_attention(q, k, v, f_logit, b_f):
    B, S, H, _ = q.shape
    scale = HEAD_DIM ** -0.5
    kpos = jnp.arange(S)
    log_f = jax.nn.log_sigmoid(f_logit.astype(jnp.float32) + b_f.astype(jnp.float32))
    c = jnp.transpose(jnp.cumsum(log_f, axis=1), (0, 2, 1))

    def block(i):
        q0 = i * QBLK
        qb = lax.dynamic_slice_in_dim(q, q0, QBLK, axis=1)
        cq = lax.dynamic_slice_in_dim(c, q0, QBLK, axis=2)
        mask = (q0 + jnp.arange(QBLK))[:, None] >= kpos[None, :]
        s = (jnp.einsum('bqhd,bkhd->bhqk', qb, k).astype(jnp.float32) * scale
             + cq[..., :, None] - c[..., None, :])
        p = masked_softmax(s, mask)
        return jnp.einsum('bhqk,bkhd->bqhd', p.astype(v.dtype), v)

    out = lax.map(block, jnp.arange(S // QBLK))
    return jnp.moveaxis(out, 0, 1).reshape(B, S, H, HEAD_DIM)


def compress(kv, pos, w1, w2):
    B, S, G, Dh = kv.shape
    n_cmp = (S - CMP_BLOCK) // CMP_STRIDE + 1
    idx = CMP_STRIDE * jnp.arange(n_cmp)[:, None] + jnp.arange(CMP_BLOCK)[None, :]
    blocks = kv[:, idx] + pos[None, None, :, None, :]
    flat = jnp.transpose(blocks, (0, 1, 3, 2, 4)).reshape(B, n_cmp, G, CMP_BLOCK * Dh)
    return jax.nn.gelu(flat @ w1) @ w2


def nsa_attention(q, k_cmp, v_cmp, k_slc, v_slc, k_win, v_win, gates,
                  pos_k, w1_k, w2_k, pos_v, w1_v, w2_v, bias_table):
    B, S = q.shape[:2]
    G, HPG, Dh = NSA_GROUPS, NSA_HPG, HEAD_DIM
    scale = Dh ** -0.5
    kc = compress(k_cmp, pos_k, w1_k, w2_k)
    vc = compress(v_cmp, pos_v, w1_v, w2_v)
    n_cmp = kc.shape[1]
    cmp_start = CMP_STRIDE * jnp.arange(n_cmp)
    cmp_end = cmp_start + CMP_BLOCK - 1
    n_sel = S // SEL_BLOCK
    top_k = min(SEL_TOPK, n_sel)
    sel_idx = jnp.arange(n_sel)
    sel_start = SEL_BLOCK * sel_idx
    overlap = jnp.clip(jnp.minimum(cmp_start[:, None] + CMP_BLOCK, sel_start[None, :] + SEL_BLOCK)
                       - jnp.maximum(cmp_start[:, None], sel_start[None, :]), 0, None
                       ).astype(jnp.float32) / CMP_BLOCK
    k_slc_t = jnp.transpose(k_slc, (0, 2, 1, 3))
    v_slc_t = jnp.transpose(v_slc, (0, 2, 1, 3))
    pad = ((0, 0), (WINDOW, 0), (0, 0), (0, 0))
    k_win_p = jnp.pad(k_win, pad)
    v_win_p = jnp.pad(v_win, pad)
    bias_g = bias_table.astype(jnp.float32).reshape(NUM_BUCKETS, G, HPG)
    b_ix = jnp.arange(B)[:, None, None]
    g_ix = jnp.arange(G)[None, :, None]

    def block(i):
        q0 = i * QBLK
        t = q0 + jnp.arange(QBLK)
        qb = lax.dynamic_slice_in_dim(q, q0, QBLK, axis=1).reshape(B, QBLK, G, HPG, Dh)
        gb = lax.dynamic_slice_in_dim(gates, q0, QBLK, axis=1).reshape(B, QBLK, 3, G, HPG)
        s_c = jnp.einsum('bqghd,bngd->bghqn', qb, kc).astype(jnp.float32) * scale
        m_c = cmp_end[None, :] <= t[:, None]
        p_c = masked_softmax(s_c, m_c) * jnp.any(m_c, axis=-1)[:, None].astype(jnp.float32)
        o_c = jnp.einsum('bghqn,bngd->bqghd', p_c.astype(vc.dtype), vc)
        imp = jnp.einsum('bghqn,nj->bgqj', p_c, overlap)
        cur = t // SEL_BLOCK
        forced = ((sel_idx[None, :] == 0) | (sel_idx[None, :] == cur[:, None])
                  | (sel_idx[None, :] == cur[:, None] - 1))
        valid = sel_start[None, :] <= t[:, None]
        score = jnp.where(valid, imp + jnp.where(forced, SEL_FORCE, 0.0), NEG_INF)
        _, blk = lax.top_k(score, top_k)
        tok = (blk[..., None] * SEL_BLOCK + jnp.arange(SEL_BLOCK)).reshape(B, G, -1)
        ks = k_slc_t[b_ix, g_ix, tok].reshape(B, G, QBLK, top_k * SEL_BLOCK, Dh)
        vs = v_slc_t[b_ix, g_ix, tok].reshape(B, G, QBLK, top_k * SEL_BLOCK, Dh)
        rel_s = t[None, None, :, None] - tok.reshape(B, G, QBLK, top_k * SEL_BLOCK)
        b_s = jnp.moveaxis(bias_g[t5_bucket(rel_s), jnp.arange(G)[None, :, None, None]], -1, 2)
        s_s = jnp.einsum('bqghd,bgqkd->bghqk', qb, ks).astype(jnp.float32) * scale + b_s
        p_s = masked_softmax(s_s, (rel_s >= 0)[:, :, None])
        o_s = jnp.einsum('bghqk,bgqkd->bqghd', p_s.astype(vs.dtype), vs)
        kw = lax.dynamic_slice_in_dim(k_win_p, q0, WINDOW + QBLK, axis=1)
        vw = lax.dynamic_slice_in_dim(v_win_p, q0, WINDOW + QBLK, axis=1)
        s_pos = q0 - WINDOW + jnp.arange(WINDOW + QBLK)
        rel_w = t[:, None] - s_pos[None, :]
        m_w = (rel_w >= 0) & (rel_w < WINDOW) & (s_pos[None, :] >= 0)
        b_w = jnp.transpose(bias_g[t5_bucket(rel_w)], (2, 3, 0, 1))
        s_w = jnp.einsum('bqghd,bkgd->bghqk', qb, kw).astype(jnp.float32) * scale + b_w
        p_w = masked_softmax(s_w, m_w)
        o_w = jnp.einsum('bghqk,bkgd->bqghd', p_w.astype(vw.dtype), vw)
        o = (gb[:, :, 0, :, :, None] * o_c + gb[:, :, 1, :, :, None] * o_s
             + gb[:, :, 2, :, :, None] * o_w)
        return o.reshape(B, QBLK, G * HPG * Dh)

    out = lax.map(block, jnp.arange(S // QBLK))
    return jnp.moveaxis(out, 0, 1).reshape(B, S, G * HPG * Dh)


def hier_moe(h, wg, bg, we, be, w_gate, w_up, w_down):
    B, S, D = h.shape
    N = B * S
    xt = h.reshape(N, D)
    g_prob = jax.nn.softmax((xt @ wg).astype(jnp.float32) + bg.astype(jnp.float32), axis=-1)
    g_p, g_idx = lax.top_k(g_prob, 1)
    e_logits = ((xt @ we).astype(jnp.float32) + be.astype(jnp.float32)).reshape(N, N_GROUPS, EXPERTS_PER_GROUP)
    e_in = jnp.take_along_axis(e_logits, g_idx[:, :, None], axis=1)[:, 0]
    e_val, e_loc = lax.top_k(e_in, TOP_K_EXPERTS)
    weight = (g_p * jax.nn.softmax(e_val, axis=-1)).astype(h.dtype)
    eid = g_idx * EXPERTS_PER_GROUP + e_loc
    nk = N * TOP_K_EXPERTS
    flat_e = eid.reshape(-1)
    flat_w = weight.reshape(-1)
    flat_tok = jnp.repeat(jnp.arange(N), TOP_K_EXPERTS)
    order = jnp.argsort(flat_e)
    se, stok, sw = flat_e[order], flat_tok[order], flat_w[order]
    counts = jnp.bincount(flat_e, length=N_EXPERTS)
    start = jnp.cumsum(counts) - counts
    padded = (counts + MOE_CHUNK - 1) // MOE_CHUNK * MOE_CHUNK
    pend = jnp.cumsum(padded)
    dest = (pend - padded)[se] + jnp.arange(nk) - start[se]
    n_chunks = -(-nk // MOE_CHUNK) + N_EXPERTS
    P = n_chunks * MOE_CHUNK
    slot_tok = jnp.zeros((P,), jnp.int32).at[dest].set(stok)
    slot_w = jnp.zeros((P,), h.dtype).at[dest].set(sw)
    chunk_e = jnp.minimum(jnp.searchsorted(pend, jnp.arange(n_chunks) * MOE_CHUNK, side='right'),
                          N_EXPERTS - 1)
    xbuf = xt[slot_tok].reshape(n_chunks, MOE_CHUNK, D)

    def expert_chunk(args):
        xc, e = args
        return (jax.nn.silu(xc @ w_gate[e]) * (xc @ w_up[e])) @ w_down[e]

    ybuf = lax.map(expert_chunk, (xbuf, chunk_e)).reshape(P, D)
    y = jnp.zeros((N, D), h.dtype).at[slot_tok].add(ybuf * slot_w[:, None])
    return y.reshape(B, S, D)


def setup_inputs(seed: int = 0) -> dict:
    key = jax.random.key(seed)
    ks = jax.random.split(key, 24)

    def nrm(k, shape, scale):
        return jax.random.normal(k, shape, jnp.float32) * scale

    return {
        "x": nrm(ks[0], (BATCH, SEQ, D_MODEL), 1.0),
        "ln1": 1.0 + nrm(ks[1], (DEPTH, D_MODEL), 0.02),
        "w_in": nrm(ks[2], (DEPTH, D_MODEL, D_IN), D_MODEL ** -0.5),
        "diff_lambda": nrm(ks[3], (DEPTH, 4, HEAD_DIM), 0.1),
        "diff_subln": 1.0 + nrm(ks[4], (DEPTH, 2 * HEAD_DIM), 0.02),
        "fox_bf": 1.0 + 3.0 * jax.random.uniform(ks[5], (DEPTH, FOX_HEADS), jnp.float32),
        "cmp_pos_k": nrm(ks[6], (DEPTH, CMP_BLOCK, HEAD_DIM), 0.1),
        "cmp_w1_k": nrm(ks[7], (DEPTH, CMP_BLOCK * HEAD_DIM, CMP_HIDDEN), (CMP_BLOCK * HEAD_DIM) ** -0.5),
        "cmp_w2_k": nrm(ks[8], (DEPTH, CMP_HIDDEN, HEAD_DIM), CMP_HIDDEN ** -0.5),
        "cmp_pos_v": nrm(ks[9], (DEPTH, CMP_BLOCK, HEAD_DIM), 0.1),
        "cmp_w1_v": nrm(ks[10], (DEPTH, CMP_BLOCK * HEAD_DIM, CMP_HIDDEN), (CMP_BLOCK * HEAD_DIM) ** -0.5),
        "cmp_w2_v": nrm(ks[11], (DEPTH, CMP_HIDDEN, HEAD_DIM), CMP_HIDDEN ** -0.5),
        "t5_table": nrm(ks[12], (NUM_BUCKETS, DIFF_HEADS + NSA_HEADS), 0.5),
        "w_out": nrm(ks[13], (DEPTH, D_MIX, D_MODEL), D_MIX ** -0.5),
        "ln2": 1.0 + nrm(ks[14], (DEPTH, D_MODEL), 0.02),
        "router_group_w": nrm(ks[15], (DEPTH, D_MODEL, N_GROUPS), D_MODEL ** -0.5),
        "router_group_b": nrm(ks[16], (DEPTH, N_GROUPS), 0.01),
        "router_expert_w": nrm(ks[17], (DEPTH, D_MODEL, N_EXPERTS), D_MODEL ** -0.5),
        "router_expert_b": nrm(ks[18], (DEPTH, N_EXPERTS), 0.01),
        "w_gate": nrm(ks[19], (DEPTH, N_EXPERTS, D_MODEL, D_EXPERT), D_MODEL ** -0.5),
        "w_up": nrm(ks[20], (DEPTH, N_EXPERTS, D_MODEL, D_EXPERT), D_MODEL ** -0.5),
        "w_down": nrm(ks[21], (DEPTH, N_EXPERTS, D_EXPERT, D_MODEL), D_EXPERT ** -0.5),
        "ln_f": 1.0 + nrm(ks[22], (D_MODEL,), 0.02),
    }


def reference(x, ln1, w_in, diff_lambda, diff_subln, fox_bf, cmp_pos_k, cmp_w1_k, cmp_w2_k,
              cmp_pos_v, cmp_w1_v, cmp_w2_v, t5_table, w_out, ln2, router_group_w,
              router_group_b, router_expert_w, router_expert_b, w_gate, w_up, w_down, ln_f):
    B, S, _ = x.shape
    split_points = [int(v) for v in np.cumsum(SPLIT_SIZES)[:-1]]
    t5_a = t5_table[:, :DIFF_HEADS]
    t5_c = t5_table[:, DIFF_HEADS:]
    for l in range(DEPTH):
        h = rms_norm(x, ln1[l])
        proj = h @ w_in[l]
        (a_q, a_k, a_v, b_q, b_k, b_v, b_f, c_q, c_kc, c_vc, c_ks, c_vs, c_kw, c_vw,
         c_g) = jnp.split(proj, split_points, axis=-1)
        dl = diff_lambda[l].astype(jnp.float32)
        lam_init = 0.8 - 0.6 * math.exp(-0.3 * l)
        lam = jnp.exp(jnp.sum(dl[0] * dl[1])) - jnp.exp(jnp.sum(dl[2] * dl[3])) + lam_init
        o_a = diff_attention(a_q.reshape(B, S, DIFF_HEADS, 2, HEAD_DIM),
                             a_k.reshape(B, S, DIFF_HEADS, 2, HEAD_DIM),
                             a_v.reshape(B, S, DIFF_HEADS, 2 * HEAD_DIM), lam, t5_a)
        o_a = rms_norm(o_a, diff_subln[l]) * (1.0 - lam_init)
        o_b = forgetting_attention(b_q.reshape(B, S, FOX_HEADS, HEAD_DIM),
                                   b_k.reshape(B, S, FOX_HEADS, HEAD_DIM),
                                   b_v.reshape(B, S, FOX_HEADS, HEAD_DIM), b_f, fox_bf[l])
        kv_shape = (B, S, NSA_GROUPS, HEAD_DIM)
        o_c = nsa_attention(c_q.reshape(B, S, NSA_HEADS, HEAD_DIM),
                            c_kc.reshape(kv_shape), c_vc.reshape(kv_shape),
                            c_ks.reshape(kv_shape), c_vs.reshape(kv_shape),
                            c_kw.reshape(kv_shape), c_vw.reshape(kv_shape),
                            jax.nn.sigmoid(c_g.reshape(B, S, 3, NSA_HEADS)),
                            cmp_pos_k[l], cmp_w1_k[l], cmp_w2_k[l],
                            cmp_pos_v[l], cmp_w1_v[l], cmp_w2_v[l], t5_c)
        mix = jnp.concatenate([o_a.reshape(B, S, -1), o_b.reshape(B, S, -1), o_c], axis=-1)
        x = x + mix @ w_out[l]
        h = rms_norm(x, ln2[l])
        x = x + hier_moe(h, router_group_w[l], router_group_b[l], router_expert_w[l],
                         router_expert_b[l], w_gate[l], w_up[l], w_down[l])
    return rms_norm(x, ln_f)
```

```python
import functools
import math

import numpy as np
import jax
import jax.numpy as jnp
from jax import lax
from jax.experimental import pallas as pl
from jax.experimental.pallas import tpu as pltpu

F32 = jnp.float32
BF16 = jnp.bfloat16
I32 = jnp.int32

D_MODEL = 2048
HEAD_DIM = 64
DIFF_HEADS = 4
FOX_HEADS = 8
NSA_HEADS = 16
NSA_GROUPS = 4
NSA_HPG = 4
CMP_BLOCK = 32
CMP_STRIDE = 16
CMP_HIDDEN = 256
SEL_BLOCK = 64
SEL_TOPK = 16
WINDOW = 512
NUM_BUCKETS = 32
MAX_DISTANCE = 128
N_GROUPS = 8
EXPERTS_PER_GROUP = 8
N_EXPERTS = 64
D_EXPERT = 256
MOE_CHUNK = 128
RMS_EPS = 1e-6
NEG_INF = -1e30
SEL_FORCE = 1e4
SCALE = HEAD_DIM ** -0.5

N_MAIN = 5632
B_FOX_F = 3072
B_NSA_G = 5640
GATE_COLS = 128
SEL_LANES = 128
SEL_MASKED = -32768.0
PICKED = -3.0e38

OFF_AQ, OFF_AK, OFF_AV = 0, 8, 16
OFF_BQ, OFF_BK, OFF_BV = 24, 32, 40
OFF_CQ = 48
OFF_CKC, OFF_CVC, OFF_CKS, OFF_CVS, OFF_CKW, OFF_CVW = 64, 68, 72, 76, 80, 84

VMEM_LIMIT = 56 * 1024 * 1024


def _cparams(n_axes):
    return pltpu.CompilerParams(dimension_semantics=("arbitrary",) * n_axes,
                                vmem_limit_bytes=VMEM_LIMIT)


IN_TM = 1024
IN_TN = 512


def _inproj_kernel(x_ref, g_ref, w_ref, wg_ref, o64_ref, og_ref, h_ref):
    j = pl.program_id(1)

    @pl.when(j == 0)
    def _():
        x = x_ref[...]
        ms = jnp.mean(x * x, axis=-1, keepdims=True)
        h_ref[...] = (x * lax.rsqrt(ms + RMS_EPS) * g_ref[...]).astype(BF16)

    res = jnp.dot(h_ref[...], w_ref[...], preferred_element_type=F32)
    for b in range(IN_TN // HEAD_DIM):
        o64_ref[b] = res[:, b * HEAD_DIM:(b + 1) * HEAD_DIM].astype(BF16)

    @pl.when(j == pl.num_programs(1) - 1)
    def _():
        og_ref[...] = jnp.dot(h_ref[...], wg_ref[...], preferred_element_type=F32)


def _inproj(x, g, w_main, w_gate):
    S = x.shape[0]
    tm = min(IN_TM, S)
    nj = N_MAIN // IN_TN
    return pl.pallas_call(
        _inproj_kernel,
        grid=(S // tm, nj),
        in_specs=[
            pl.BlockSpec((tm, D_MODEL), lambda i, j: (i, 0)),
            pl.BlockSpec((1, D_MODEL), lambda i, j: (0, 0)),
            pl.BlockSpec((D_MODEL, IN_TN), lambda i, j: (0, j)),
            pl.BlockSpec((D_MODEL, GATE_COLS), lambda i, j: (0, 0)),
        ],
        out_specs=[
            pl.BlockSpec((IN_TN // HEAD_DIM, tm, HEAD_DIM), lambda i, j: (j, i, 0)),
            pl.BlockSpec((tm, GATE_COLS), lambda i, j: (i, 0)),
        ],
        out_shape=[
            jax.ShapeDtypeStruct((N_MAIN // HEAD_DIM, S, HEAD_DIM), BF16),
            jax.ShapeDtypeStruct((S, GATE_COLS), F32),
        ],
        scratch_shapes=[pltpu.VMEM((tm, D_MODEL), BF16)],
        compiler_params=_cparams(2),
        name="inproj",
    )(x, g, w_main, w_gate)


CUM_TB = 512


def _split3(x):
    hi = x.astype(BF16)
    r1 = x - hi.astype(F32)
    mid = r1.astype(BF16)
    lo = (r1 - mid.astype(F32)).astype(BF16)
    return hi, mid, lo


def _cumsum_kernel(gate_ref, bf_ref, c_ref, ct_ref, carry_ref):
    i = pl.program_id(0)

    @pl.when(i == 0)
    def _():
        carry_ref[...] = jnp.zeros_like(carry_ref)

    z = gate_ref[...] + bf_ref[...]
    log_f = -(jnp.maximum(-z, 0.0) + jnp.log1p(jnp.exp(-jnp.abs(z))))
    tb = log_f.shape[0]
    row = lax.broadcasted_iota(I32, (tb, tb), 0)
    col = lax.broadcasted_iota(I32, (tb, tb), 1)
    tri = jnp.where(row >= col, 1.0, 0.0).astype(BF16)
    hi, mid, lo = _split3(log_f)
    c = (jnp.dot(tri, hi, preferred_element_type=F32)
         + jnp.dot(tri, mid, preferred_element_type=F32)
         + jnp.dot(tri, lo, preferred_element_type=F32)) + carry_ref[...]
    c_ref[...] = c
    ct_ref[...] = c.T[0:FOX_HEADS, :]
    carry_ref[...] = c[tb - 1:tb, :]


def _fox_cumsum(gate, bf_row):
    S = gate.shape[0]
    tb = min(CUM_TB, S)
    return pl.pallas_call(
        _cumsum_kernel,
        grid=(S // tb,),
        in_specs=[pl.BlockSpec((tb, GATE_COLS), lambda i: (i, 0)),
                  pl.BlockSpec((1, GATE_COLS), lambda i: (0, 0))],
        out_specs=[pl.BlockSpec((tb, GATE_COLS), lambda i: (i, 0)),
                   pl.BlockSpec((FOX_HEADS, tb), lambda i: (0, i))],
        out_shape=[jax.ShapeDtypeStruct((S, GATE_COLS), F32),
                   jax.ShapeDtypeStruct((FOX_HEADS, S), F32)],
        scratch_shapes=[pltpu.VMEM((1, GATE_COLS), F32)],
        compiler_params=_cparams(1),
        name="fox_cumsum",
    )(gate, bf_row)


def _tri_pairs(nq):
    qi = np.concatenate([np.full(i + 1, i, np.int32) for i in range(nq)])
    ki = np.concatenate([np.arange(i + 1, dtype=np.int32) for i in range(nq)])
    return jnp.asarray(qi), jnp.asarray(ki)


def _online_update(s, v, m_ref, l_ref, acc_ref):
    m_prev = m_ref[...]
    m_new = jnp.maximum(m_prev, jnp.max(s, axis=1, keepdims=True))
    alpha = jnp.exp(m_prev - m_new)
    p = jnp.exp(s - m_new)
    l_ref[...] = alpha * l_ref[...] + jnp.sum(p, axis=1, keepdims=True)
    acc_ref[...] = alpha * acc_ref[...] + jnp.dot(p.astype(BF16), v, preferred_element_type=F32)
    m_ref[...] = m_new


def _init_state(m_ref, l_ref, acc_ref):
    m_ref[...] = jnp.full_like(m_ref, NEG_INF)
    l_ref[...] = jnp.zeros_like(l_ref)
    acc_ref[...] = jnp.zeros_like(acc_ref)


def _qk(q, k):
    return lax.dot_general(q, k, (((1,), (1,)), ((), ())), preferred_element_type=F32)


def _t5_bucket(rel):
    n = jnp.maximum(rel, 0)
    max_exact = NUM_BUCKETS // 2
    nf = jnp.maximum(n, max_exact).astype(F32)
    large = max_exact + (jnp.log(nf / max_exact) / math.log(MAX_DISTANCE / max_exact)
                         * (NUM_BUCKETS - max_exact)).astype(I32)
    return jnp.where(n < max_exact, n, jnp.minimum(large, NUM_BUCKETS - 1))


def _bias_tiles(table, T):
    assert T + 1 >= MAX_DISTANCE
    i = jnp.arange(T)[:, None]
    j = jnp.arange(T)[None, :]
    rel = jnp.stack([i - j, T + i - j])
    t = table.astype(F32)
    b = t[_t5_bucket(rel)] - t[NUM_BUCKETS - 1]
    b = jnp.where((rel >= 0)[..., None], b, NEG_INF)
    return jnp.transpose(b, (3, 0, 1, 2))


A_T = 512


def _diff_kernel(qt_ref, kt_ref, q1_ref, q2_ref, k1_ref, k2_ref, v_ref, bias_ref, dl_ref, sub_ref,
                 cst_ref, o_ref, qs1, qs2, m1, l1, a1, m2, l2, a2):
    p = pl.program_id(1)
    qi = qt_ref[p]
    ki = kt_ref[p]

    @pl.when(ki == 0)
    def _():
        _init_state(m1, l1, a1)
        _init_state(m2, l2, a2)
        qs1[...] = q1_ref[0] * SCALE
        qs2[...] = q2_ref[0] * SCALE

    def step(with_bias):
        s1 = _qk(qs1[...], k1_ref[0])
        s2 = _qk(qs2[...], k2_ref[0])
        if with_bias:
            b = bias_ref[0, 0]
            s1 = s1 + b
            s2 = s2 + b
        v = v_ref[0]
        _online_update(s1, v, m1, l1, a1)
        _online_update(s2, v, m2, l2, a2)

    @pl.when(qi - ki <= 1)
    def _():
        step(True)

    @pl.when(qi - ki > 1)
    def _():
        step(False)

    @pl.when(ki == qi)
    def _():
        dl = dl_ref[...]
        lam_init = cst_ref[0:1, 0:1]
        lam = (jnp.exp(jnp.sum(dl[0:1] * dl[1:2], axis=1, keepdims=True))
               - jnp.exp(jnp.sum(dl[2:3] * dl[3:4], axis=1, keepdims=True)) + lam_init)
        o = a1[...] / l1[...] - lam * (a2[...] / l2[...])
        ms = jnp.mean(o * o, axis=-1, keepdims=True)
        y = o * lax.rsqrt(ms + RMS_EPS) * sub_ref[...]
        o_ref[0] = (y * (1.0 - lam_init)).astype(BF16)


def _diff_attention(p64, v128, bias, dl, subln, cst):
    S = p64.shape[1]
    T = min(A_T, S)
    nq = S // T
    qt, kt = _tri_pairs(nq)
    grid_spec = pltpu.PrefetchScalarGridSpec(
        num_scalar_prefetch=2,
        grid=(DIFF_HEADS, int(qt.shape[0])),
        in_specs=[
            pl.BlockSpec((1, T, HEAD_DIM), lambda h, p, qt, kt: (OFF_AQ + 2 * h, qt[p], 0)),
            pl.BlockSpec((1, T, HEAD_DIM), lambda h, p, qt, kt: (OFF_AQ + 2 * h + 1, qt[p], 0)),
            pl.BlockSpec((1, T, HEAD_DIM), lambda h, p, qt, kt: (OFF_AK + 2 * h, kt[p], 0)),
            pl.BlockSpec((1, T, HEAD_DIM), lambda h, p, qt, kt: (OFF_AK + 2 * h + 1, kt[p], 0)),
            pl.BlockSpec((1, T, 2 * HEAD_DIM), lambda h, p, qt, kt: (h, kt[p], 0)),
            pl.BlockSpec((1, 1, T, T), lambda h, p, qt, kt: (h, jnp.minimum(qt[p] - kt[p], 1), 0, 0)),
            pl.BlockSpec((4, HEAD_DIM), lambda h, p, qt, kt: (0, 0)),
            pl.BlockSpec((1, 2 * HEAD_DIM), lambda h, p, qt, kt: (0, 0)),
            pl.BlockSpec((1, 128), lambda h, p, qt, kt: (0, 0)),
        ],
        out_specs=pl.BlockSpec((1, T, 2 * HEAD_DIM), lambda h, p, qt, kt: (h, qt[p], 0)),
        scratch_shapes=[
            pltpu.VMEM((T, HEAD_DIM), BF16), pltpu.VMEM((T, HEAD_DIM), BF16),
            pltpu.VMEM((T, 1), F32), pltpu.VMEM((T, 1), F32), pltpu.VMEM((T, 2 * HEAD_DIM), F32),
            pltpu.VMEM((T, 1), F32), pltpu.VMEM((T, 1), F32), pltpu.VMEM((T, 2 * HEAD_DIM), F32),
        ],
    )
    return pl.pallas_call(
        _diff_kernel,
        grid_spec=grid_spec,
        out_shape=jax.ShapeDtypeStruct((DIFF_HEADS, S, 2 * HEAD_DIM), BF16),
        compiler_params=_cparams(2),
        name="diff_attn",
    )(qt, kt, p64, p64, p64, p64, v128, bias, dl, subln, cst)


B_T = 512


def _fox_kernel(qt_ref, kt_ref, q_ref, k_ref, v_ref, c_ref, ct_ref, o_ref, qs, cq, m, l, acc):
    h = pl.program_id(0)
    p = pl.program_id(1)
    qi = qt_ref[p]
    ki = kt_ref[p]

    @pl.when(ki == 0)
    def _():
        _init_state(m, l, acc)
        qs[...] = q_ref[0] * SCALE
        c = c_ref[...]
        lane = lax.broadcasted_iota(I32, c.shape, 1)
        cq[...] = jnp.sum(jnp.where(lane == h, c, 0.0), axis=1, keepdims=True)

    def step(diag):
        s = _qk(qs[...], k_ref[0]) + cq[...] - ct_ref[pl.ds(h, 1), :]
        if diag:
            row = lax.broadcasted_iota(I32, s.shape, 0)
            col = lax.broadcasted_iota(I32, s.shape, 1)
            s = jnp.where(row >= col, s, NEG_INF)
        _online_update(s, v_ref[0], m, l, acc)

    @pl.when(ki == qi)
    def _():
        step(True)

    @pl.when(ki < qi)
    def _():
        step(False)

    @pl.when(ki == qi)
    def _():
        o_ref[0] = (acc[...] / l[...]).astype(BF16)


def _fox_attention(p64, c, ct):
    S = p64.shape[1]
    T = min(B_T, S)
    nq = S // T
    qt, kt = _tri_pairs(nq)
    grid_spec = pltpu.PrefetchScalarGridSpec(
        num_scalar_prefetch=2,
        grid=(FOX_HEADS, int(qt.shape[0])),
        in_specs=[
            pl.BlockSpec((1, T, HEAD_DIM), lambda h, p, qt, kt: (OFF_BQ + h, qt[p], 0)),
            pl.BlockSpec((1, T, HEAD_DIM), lambda h, p, qt, kt: (OFF_BK + h, kt[p], 0)),
            pl.BlockSpec((1, T, HEAD_DIM), lambda h, p, qt, kt: (OFF_BV + h, kt[p], 0)),
            pl.BlockSpec((T, GATE_COLS), lambda h, p, qt, kt: (qt[p], 0)),
            pl.BlockSpec((FOX_HEADS, T), lambda h, p, qt, kt: (0, kt[p])),
        ],
        out_specs=pl.BlockSpec((1, T, HEAD_DIM), lambda h, p, qt, kt: (h, qt[p], 0)),
        scratch_shapes=[
            pltpu.VMEM((T, HEAD_DIM), BF16), pltpu.VMEM((T, 1), F32),
            pltpu.VMEM((T, 1), F32), pltpu.VMEM((T, 1), F32), pltpu.VMEM((T, HEAD_DIM), F32),
        ],
    )
    return pl.pallas_call(
        _fox_kernel,
        grid_spec=grid_spec,
        out_shape=jax.ShapeDtypeStruct((FOX_HEADS, S, HEAD_DIM), BF16),
        compiler_params=_cparams(2),
        name="fox_attn",
    )(qt, kt, p64, p64, p64, c, ct)


def _gelu_tanh(x):
    return 0.5 * x * (1.0 + jnp.tanh(math.sqrt(2.0 / math.pi) * (x + 0.044715 * (x * x * x))))


def _compress_kernel(x_ref, pos_ref, w1_ref, w2_ref, o_ref):
    x = x_ref[0].astype(F32)
    half = x.shape[1]
    pos = pos_ref[0]
    x_lo = (x + pos[0:1]).astype(BF16)
    x_hi = (x + pos[1:2]).astype(BF16)
    w1 = w1_ref[0]
    y_lo = jnp.dot(x_lo, w1[:half], preferred_element_type=F32)
    y_hi = jnp.dot(x_hi, w1[half:], preferred_element_type=F32)
    n = y_hi.shape[0]
    pre = y_lo + pltpu.roll(y_hi, n - 1, 0)
    o_ref[0] = jnp.dot(_gelu_tanh(pre).astype(BF16), w2_ref[0], preferred_element_type=F32).astype(BF16)


def _compress(chunks, pos, w1, w2):
    n = chunks.shape[1]
    half = CMP_STRIDE * HEAD_DIM
    return pl.pallas_call(
        _compress_kernel,
        grid=(2 * NSA_GROUPS,),
        in_specs=[
            pl.BlockSpec((1, n, half), lambda i: (i, 0, 0)),
            pl.BlockSpec((1, 2, half), lambda i: (i // NSA_GROUPS, 0, 0)),
            pl.BlockSpec((1, 2 * half, CMP_HIDDEN), lambda i: (i // NSA_GROUPS, 0, 0)),
            pl.BlockSpec((1, CMP_HIDDEN, HEAD_DIM), lambda i: (i // NSA_GROUPS, 0, 0)),
        ],
        out_specs=pl.BlockSpec((1, n, HEAD_DIM), lambda i: (i, 0, 0)),
        out_shape=jax.ShapeDtypeStruct((2 * NSA_GROUPS, n, HEAD_DIM), BF16),
        compiler_params=_cparams(1),
        name="nsa_compress",
    )(chunks, pos, w1, w2)


C_T = 256


def _cmp_kernel(q_ref, kc_ref, vc_ref, ks_ref, ov_ref, oc_ref, qa_ref, ka_ref):
    qi = pl.program_id(1)
    T = q_ref.shape[1]
    ncp = kc_ref.shape[1]
    q0 = qi * T
    qs = (q_ref[...] * SCALE).reshape(NSA_HPG * T, HEAD_DIM)
    s = _qk(qs, kc_ref[0]).reshape(NSA_HPG, T, ncp)
    t = q0 + lax.broadcasted_iota(I32, (T, ncp), 0)
    n_idx = lax.broadcasted_iota(I32, (T, ncp), 1)
    visible = (CMP_STRIDE * n_idx + (CMP_BLOCK - 1) <= t)[None]
    s = jnp.where(visible, s, NEG_INF)
    smax = jnp.max(s, axis=2, keepdims=True)
    e = jnp.where(visible, jnp.exp(s - smax), 0.0)
    den = jnp.sum(e, axis=2, keepdims=True)
    p = e / jnp.where(den > 0.0, den, 1.0)
    o = jnp.dot(p.reshape(NSA_HPG * T, ncp).astype(BF16), vc_ref[0], preferred_element_type=F32)
    oc_ref[...] = o.reshape(NSA_HPG, T, HEAD_DIM).astype(BF16)

    psum = p[0] + p[1] + p[2] + p[3]
    hi, mid, lo = _split3(psum)
    ov = ov_ref[...]
    imp = (jnp.dot(hi, ov, preferred_element_type=F32) + jnp.dot(mid, ov, preferred_element_type=F32)
           + jnp.dot(lo, ov, preferred_element_type=F32))
    tq = q0 + lax.broadcasted_iota(I32, (T, SEL_LANES), 0)
    jb = lax.broadcasted_iota(I32, (T, SEL_LANES), 1)
    cur = tq // SEL_BLOCK
    forced = (jb == 0) | (jb == cur) | (jb == cur - 1)
    valid = SEL_BLOCK * jb <= tq
    score = jnp.where(valid, imp + jnp.where(forced, SEL_FORCE, 0.0), NEG_INF)
    sc = score.T
    jrow = lax.broadcasted_iota(I32, sc.shape, 0).astype(F32)
    sel = jnp.zeros(sc.shape, F32)
    for _ in range(SEL_TOPK):
        mx = jnp.max(sc, axis=0, keepdims=True)
        first = jnp.min(jnp.where(sc == mx, jrow, float(SEL_LANES)), axis=0, keepdims=True)
        pick = jrow == first
        sel = jnp.where(pick, 1.0, sel)
        sc = jnp.where(pick, PICKED, sc)
    sel_neg = jnp.where(sel.T > 0.5, 0.0, SEL_MASKED).astype(BF16)

    zeros = jnp.zeros((T, HEAD_DIM), BF16)
    for hh in range(NSA_HPG):
        qa_ref[hh, :, 0:HEAD_DIM] = qs[hh * T:(hh + 1) * T]
        qa_ref[hh, :, HEAD_DIM:2 * HEAD_DIM] = zeros
        qa_ref[hh, :, 2 * HEAD_DIM:] = sel_neg
    kblk = (q0 + lax.broadcasted_iota(I32, (T, SEL_LANES), 0)) // SEL_BLOCK
    onehot = jnp.where(kblk == jb, 1.0, 0.0).astype(BF16)
    ka_ref[0, :, 0:HEAD_DIM] = ks_ref[0]
    ka_ref[0, :, HEAD_DIM:2 * HEAD_DIM] = zeros
    ka_ref[0, :, 2 * HEAD_DIM:] = onehot


def _cmp_attention(p64, kvc, overlap):
    S = p64.shape[1]
    T = min(C_T, S)
    ncp = kvc.shape[1]
    aug = 2 * HEAD_DIM + SEL_LANES
    return pl.pallas_call(
        _cmp_kernel,
        grid=(NSA_GROUPS, S // T),
        in_specs=[
            pl.BlockSpec((NSA_HPG, T, HEAD_DIM), lambda g, i: (OFF_CQ // NSA_HPG + g, i, 0)),
            pl.BlockSpec((1, ncp, HEAD_DIM), lambda g, i: (g, 0, 0)),
            pl.BlockSpec((1, ncp, HEAD_DIM), lambda g, i: (NSA_GROUPS + g, 0, 0)),
            pl.BlockSpec((1, T, HEAD_DIM), lambda g, i: (OFF_CKS + g, i, 0)),
            pl.BlockSpec((ncp, SEL_LANES), lambda g, i: (0, 0)),
        ],
        out_specs=[
            pl.BlockSpec((NSA_HPG, T, HEAD_DIM), lambda g, i: (g, i, 0)),
            pl.BlockSpec((NSA_HPG, T, aug), lambda g, i: (g, i, 0)),
            pl.BlockSpec((1, T, aug), lambda g, i: (g, i, 0)),
        ],
        out_shape=[
            jax.ShapeDtypeStruct((NSA_HEADS, S, HEAD_DIM), BF16),
            jax.ShapeDtypeStruct((NSA_HEADS, S, aug), BF16),
            jax.ShapeDtypeStruct((NSA_GROUPS, S, aug), BF16),
        ],
        compiler_params=_cparams(2),
        name="nsa_cmp_attn",
    )(p64, kvc, kvc, p64, overlap)


def _overlap_matrix(ncp):
    n = np.arange(ncp)[:, None] * CMP_STRIDE
    j = np.arange(SEL_LANES)[None, :] * SEL_BLOCK
    ov = np.clip(np.minimum(n + CMP_BLOCK, j + SEL_BLOCK) - np.maximum(n, j), 0, None) / CMP_BLOCK
    ov[ncp - 1:] = 0.0
    return jnp.asarray(ov, BF16)


def _sel_kernel(qt_ref, kt_ref, qa_ref, ka_ref, v_ref, bias_ref, o_ref, m, l, acc):
    p = pl.program_id(1)
    qi = qt_ref[p]
    ki = kt_ref[p]
    T = qa_ref.shape[1]
    aug = qa_ref.shape[2]

    @pl.when(ki == 0)
    def _():
        _init_state(m, l, acc)

    def step(with_bias):
        s = _qk(qa_ref[...].reshape(NSA_HPG * T, aug), ka_ref[0])
        if with_bias:
            s = (s.reshape(NSA_HPG, T, T) + bias_ref[:, 0]).reshape(NSA_HPG * T, T)
        _online_update(s, v_ref[0], m, l, acc)

    @pl.when(qi - ki <= 1)
    def _():
        step(True)

    @pl.when(qi - ki > 1)
    def _():
        step(False)

    @pl.when(ki == qi)
    def _():
        o_ref[...] = (acc[...] / l[...]).reshape(NSA_HPG, T, HEAD_DIM).astype(BF16)


def _sel_attention(q_aug, k_aug, p64, bias):
    S = p64.shape[1]
    T = min(C_T, S)
    aug = q_aug.shape[2]
    qt, kt = _tri_pairs(S // T)
    grid_spec = pltpu.PrefetchScalarGridSpec(
        num_scalar_prefetch=2,
        grid=(NSA_GROUPS, int(qt.shape[0])),
        in_specs=[
            pl.BlockSpec((NSA_HPG, T, aug), lambda g, p, qt, kt: (g, qt[p], 0)),
            pl.BlockSpec((1, T, aug), lambda g, p, qt, kt: (g, kt[p], 0)),
            pl.BlockSpec((1, T, HEAD_DIM), lambda g, p, qt, kt: (OFF_CVS + g, kt[p], 0)),
            pl.BlockSpec((NSA_HPG, 1, T, T), lambda g, p, qt, kt: (g, jnp.minimum(qt[p] - kt[p], 1), 0, 0)),
        ],
        out_specs=pl.BlockSpec((NSA_HPG, T, HEAD_DIM), lambda g, p, qt, kt: (g, qt[p], 0)),
        scratch_shapes=[pltpu.VMEM((NSA_HPG * T, 1), F32), pltpu.VMEM((NSA_HPG * T, 1), F32),
                        pltpu.VMEM((NSA_HPG * T, HEAD_DIM), F32)],
    )
    return pl.pallas_call(
        _sel_kernel,
        grid_spec=grid_spec,
        out_shape=jax.ShapeDtypeStruct((NSA_HEADS, S, HEAD_DIM), BF16),
        compiler_params=_cparams(2),
        name="nsa_sel_attn",
    )(qt, kt, q_aug, k_aug, p64, bias)


def _win_kernel(q_ref, k_ref, v_ref, bias_ref, o_ref, qs, m, l, acc):
    qi = pl.program_id(1)
    kk = pl.program_id(2)
    T = q_ref.shape[1]

    @pl.when(kk == 0)
    def _():
        _init_state(m, l, acc)
        qs[...] = (q_ref[...] * SCALE).reshape(NSA_HPG * T, HEAD_DIM)

    def step(kind):
        s = _qk(qs[...], k_ref[0])
        if kind < 2:
            s = (s.reshape(NSA_HPG, T, T) + bias_ref[:, 0]).reshape(NSA_HPG * T, T)
        else:
            row = lax.broadcasted_iota(I32, (NSA_HPG, T, T), 1)
            col = lax.broadcasted_iota(I32, (NSA_HPG, T, T), 2)
            s = jnp.where(col > row, s.reshape(NSA_HPG, T, T), NEG_INF).reshape(NSA_HPG * T, T)
        _online_update(s, v_ref[0], m, l, acc)

    @pl.when(kk == 0)
    def _():
        step(0)

    @pl.when((kk == 1) & (qi >= 1))
    def _():
        step(1)

    @pl.when((kk == 2) & (qi >= 2))
    def _():
        step(2)

    @pl.when(kk == 2)
    def _():
        o_ref[...] = (acc[...] / l[...]).reshape(NSA_HPG, T, HEAD_DIM).astype(BF16)


def _win_attention(p64, bias):
    S = p64.shape[1]
    T = min(C_T, S)
    assert WINDOW == 2 * T
    return pl.pallas_call(
        _win_kernel,
        grid=(NSA_GROUPS, S // T, 3),
        in_specs=[
            pl.BlockSpec((NSA_HPG, T, HEAD_DIM), lambda g, i, kk: (OFF_CQ // NSA_HPG + g, i, 0)),
            pl.BlockSpec((1, T, HEAD_DIM), lambda g, i, kk: (OFF_CKW + g, jnp.maximum(i - kk, 0), 0)),
            pl.BlockSpec((1, T, HEAD_DIM), lambda g, i, kk: (OFF_CVW + g, jnp.maximum(i - kk, 0), 0)),
            pl.BlockSpec((NSA_HPG, 1, T, T), lambda g, i, kk: (g, jnp.minimum(kk, 1), 0, 0)),
        ],
        out_specs=pl.BlockSpec((NSA_HPG, T, HEAD_DIM), lambda g, i, kk: (g, i, 0)),
        out_shape=jax.ShapeDtypeStruct((NSA_HEADS, S, HEAD_DIM), BF16),
        scratch_shapes=[pltpu.VMEM((NSA_HPG * T, HEAD_DIM), BF16),
                        pltpu.VMEM((NSA_HPG * T, 1), F32), pltpu.VMEM((NSA_HPG * T, 1), F32),
                        pltpu.VMEM((NSA_HPG * T, HEAD_DIM), F32)],
        compiler_params=_cparams(3),
        name="nsa_win_attn",
    )(p64, p64, p64, bias)


OUT_TM = 256
RT_E1, RT_E2, RT_R1, RT_R2 = 0, 1, 2, 3
ROUTER_LANE0 = N_GROUPS


def _outproj_kernel(oa_ref, ob_ref, oc_ref, os_ref, ow_ref, gate_ref, x_ref, w_ref, ln_ref, wr_ref, br_ref,
                    x1_ref, h2_ref, ri_ref, rw_ref, cnt_ref, mix_ref, carry_ref):
    i = pl.program_id(0)
    tm = x_ref.shape[0]

    @pl.when(i == 0)
    def _():
        carry_ref[...] = jnp.zeros_like(carry_ref)

    for h in range(DIFF_HEADS):
        mix_ref[:, 128 * h:128 * (h + 1)] = oa_ref[h]
    for h in range(FOX_HEADS):
        c0 = 512 + HEAD_DIM * h
        mix_ref[:, c0:c0 + HEAD_DIM] = ob_ref[h]
    sig = jax.nn.sigmoid(gate_ref[...])
    for h in range(NSA_HEADS):
        c0 = 1024 + HEAD_DIM * h
        g0 = sig[:, FOX_HEADS + h:FOX_HEADS + h + 1]
        g1 = sig[:, FOX_HEADS + NSA_HEADS + h:FOX_HEADS + NSA_HEADS + h + 1]
        g2 = sig[:, FOX_HEADS + 2 * NSA_HEADS + h:FOX_HEADS + 2 * NSA_HEADS + h + 1]
        o = g0 * oc_ref[h].astype(F32) + g1 * os_ref[h].astype(F32) + g2 * ow_ref[h].astype(F32)
        mix_ref[:, c0:c0 + HEAD_DIM] = o.astype(BF16)

    x1 = x_ref[...] + jnp.dot(mix_ref[...], w_ref[...], preferred_element_type=F32)
    x1_ref[...] = x1
    ms = jnp.mean(x1 * x1, axis=-1, keepdims=True)
    h2 = x1 * lax.rsqrt(ms + RMS_EPS) * ln_ref[...]
    h2_ref[...] = h2

    a0, a1, a2 = _split3(h2)
    b0, b1, b2 = _split3(wr_ref[...])
    dot = lambda a, b: jnp.dot(a, b, preferred_element_type=F32)
    logits = (dot(a0, b0) + (dot(a0, b1) + dot(a1, b0))
              + (dot(a0, b2) + dot(a1, b1) + dot(a2, b0))) + br_ref[...]
    lane = lax.broadcasted_iota(I32, logits.shape, 1)
    lane_f = lane.astype(F32)

    def first_lane(cond):
        return jnp.min(jnp.where(cond, lane_f, 128.0), axis=1, keepdims=True).astype(I32)

    is_g = lane < N_GROUPS
    lg = jnp.where(is_g, logits, NEG_INF)
    gmax = jnp.max(lg, axis=1, keepdims=True)
    gsum = jnp.sum(jnp.where(is_g, jnp.exp(lg - gmax), 0.0), axis=1, keepdims=True)
    g_p = 1.0 / gsum
    g_idx = first_lane(lg == gmax)
    e_lane = lane - ROUTER_LANE0
    in_group = (e_lane >= 0) & (e_lane < N_EXPERTS) & ((e_lane >> 3) == g_idx)
    em = jnp.where(in_group, logits, NEG_INF)
    v1 = jnp.max(em, axis=1, keepdims=True)
    i1 = first_lane(em == v1)
    em2 = jnp.where(lane == i1, PICKED, em)
    v2 = jnp.max(em2, axis=1, keepdims=True)
    i2 = first_lane(em2 == v2)
    tt = jnp.exp(v2 - v1)
    w1 = g_p / (1.0 + tt)
    w2 = g_p * tt / (1.0 + tt)

    oh = jnp.where((lane == i1) | (lane == i2), 1.0, 0.0)
    row = lax.broadcasted_iota(I32, (tm, tm), 0)
    col = lax.broadcasted_iota(I32, (tm, tm), 1)
    tri = jnp.where(row > col, 1.0, 0.0).astype(BF16)
    prefix = jnp.dot(tri, oh.astype(BF16), preferred_element_type=F32) + carry_ref[...]
    r1 = jnp.sum(jnp.where(lane == i1, prefix, 0.0), axis=1, keepdims=True)
    r2 = jnp.sum(jnp.where(lane == i2, prefix, 0.0), axis=1, keepdims=True)
    carry = carry_ref[...] + jnp.sum(oh, axis=0, keepdims=True)
    carry_ref[...] = carry
    cnt_ref[...] = carry

    ri = jnp.where(lane == RT_E1, i1 - ROUTER_LANE0,
                   jnp.where(lane == RT_E2, i2 - ROUTER_LANE0,
                             jnp.where(lane == RT_R1, r1.astype(I32),
                                       jnp.where(lane == RT_R2, r2.astype(I32), 0))))
    ri_ref[...] = ri
    rw_ref[...] = jnp.where(lane == 0, w1, jnp.where(lane == 1, w2, 0.0))


def _outproj(o_a, o_b, o_c, o_s, o_w, gate, x, w_out, ln2, wr, br):
    S = x.shape[0]
    tm = min(OUT_TM, S)
    full = lambda shape: pl.BlockSpec(shape, lambda i: (0,) * len(shape))
    return pl.pallas_call(
        _outproj_kernel,
        grid=(S // tm,),
        in_specs=[
            pl.BlockSpec((DIFF_HEADS, tm, 2 * HEAD_DIM), lambda i: (0, i, 0)),
            pl.BlockSpec((FOX_HEADS, tm, HEAD_DIM), lambda i: (0, i, 0)),
            pl.BlockSpec((NSA_HEADS, tm, HEAD_DIM), lambda i: (0, i, 0)),
            pl.BlockSpec((NSA_HEADS, tm, HEAD_DIM), lambda i: (0, i, 0)),
            pl.BlockSpec((NSA_HEADS, tm, HEAD_DIM), lambda i: (0, i, 0)),
            pl.BlockSpec((tm, GATE_COLS), lambda i: (i, 0)),
            pl.BlockSpec((tm, D_MODEL), lambda i: (i, 0)),
            full((D_MODEL, D_MODEL)),
            full((1, D_MODEL)),
            full((D_MODEL, 128)),
            full((1, 128)),
        ],
        out_specs=[
            pl.BlockSpec((tm, D_MODEL), lambda i: (i, 0)),
            pl.BlockSpec((tm, D_MODEL), lambda i: (i, 0)),
            pl.BlockSpec((tm, 128), lambda i: (i, 0)),
            pl.BlockSpec((tm, 128), lambda i: (i, 0)),
            pl.BlockSpec((1, 128), lambda i: (0, 0)),
        ],
        out_shape=[
            jax.ShapeDtypeStruct((S, D_MODEL), F32),
            jax.ShapeDtypeStruct((S, D_MODEL), F32),
            jax.ShapeDtypeStruct((S, 128), I32),
            jax.ShapeDtypeStruct((S, 128), F32),
            jax.ShapeDtypeStruct((1, 128), F32),
        ],
        scratch_shapes=[pltpu.VMEM((tm, D_MODEL), BF16), pltpu.VMEM((1, 128), F32)],
        compiler_params=_cparams(1),
        name="outproj_router",
    )(o_a, o_b, o_c, o_s, o_w, gate, x, w_out, ln2, wr, br)


DISP_TM = 256


def _dispatch_kernel(dest_ref, h_ref, xb_in_ref, xb_ref, sem):
    del xb_in_ref
    i = pl.program_id(0)
    base = i * DISP_TM

    def copy(t, k):
        return pltpu.make_async_copy(h_ref.at[pl.ds(base + t, 1)],
                                     xb_ref.at[pl.ds(dest_ref[2 * (base + t) + k], 1)], sem)

    def start(t, c):
        copy(t, 0).start()
        copy(t, 1).start()
        return c

    def wait(t, c):
        copy(t, 0).wait()
        copy(t, 1).wait()
        return c

    lax.fori_loop(0, DISP_TM, start, 0)
    lax.fori_loop(0, DISP_TM, wait, 0)


def _dispatch(dest_flat, h2, xbuf0):
    S = h2.shape[0]
    assert S % DISP_TM == 0
    grid_spec = pltpu.PrefetchScalarGridSpec(
        num_scalar_prefetch=1,
        grid=(S // DISP_TM,),
        in_specs=[pl.BlockSpec(memory_space=pl.ANY), pl.BlockSpec(memory_space=pl.ANY)],
        out_specs=pl.BlockSpec(memory_space=pl.ANY),
        scratch_shapes=[pltpu.SemaphoreType.DMA(())],
    )
    return pl.pallas_call(
        _dispatch_kernel,
        grid_spec=grid_spec,
        out_shape=jax.ShapeDtypeStruct(xbuf0.shape, xbuf0.dtype),
        input_output_aliases={2: 0},
        compiler_params=pltpu.CompilerParams(dimension_semantics=("arbitrary",), has_side_effects=True),
        name="moe_dispatch",
    )(dest_flat, h2, xbuf0)


def _expert_kernel(ce_ref, used_ref, x_ref, wg_ref, wu_ref, wd_ref, y_ref):
    c = pl.program_id(0)

    @pl.when(c < used_ref[0])
    def _():
        x = x_ref[...].astype(BF16)
        g = jnp.dot(x, wg_ref[0].astype(BF16), preferred_element_type=F32)
        u = jnp.dot(x, wu_ref[0].astype(BF16), preferred_element_type=F32)
        hmid = (g * jax.nn.sigmoid(g) * u).astype(BF16)
        y_ref[...] = jnp.dot(hmid, wd_ref[0].astype(BF16), preferred_element_type=F32)

    @pl.when(c >= used_ref[0])
    def _():
        y_ref[...] = jnp.zeros_like(y_ref)


def _experts(chunk_e, n_used, xbuf, w_gate, w_up, w_down):
    P = xbuf.shape[0]
    n_chunks = P // MOE_CHUNK
    row_blk = lambda c, ce, nu: (jnp.minimum(c, nu[0] - 1), 0)
    wt_blk = lambda c, ce, nu: (ce[jnp.minimum(c, nu[0] - 1)], 0, 0)
    grid_spec = pltpu.PrefetchScalarGridSpec(
        num_scalar_prefetch=2,
        grid=(n_chunks,),
        in_specs=[
            pl.BlockSpec((MOE_CHUNK, D_MODEL), row_blk),
            pl.BlockSpec((1, D_MODEL, D_EXPERT), wt_blk),
            pl.BlockSpec((1, D_MODEL, D_EXPERT), wt_blk),
            pl.BlockSpec((1, D_EXPERT, D_MODEL), wt_blk),
        ],
        out_specs=pl.BlockSpec((MOE_CHUNK, D_MODEL), lambda c, ce, nu: (c, 0)),
    )
    return pl.pallas_call(
        _expert_kernel,
        grid_spec=grid_spec,
        out_shape=jax.ShapeDtypeStruct((P, D_MODEL), F32),
        compiler_params=_cparams(1),
        name="moe_experts",
    )(chunk_e, n_used, xbuf, w_gate, w_up, w_down)


COMB_TM = 256


def _combine_kernel(final, dest_ref, x_ref, rw_ref, ln_ref, y_ref, o_ref, r0, r1, sem):
    i = pl.program_id(0)
    base = i * COMB_TM

    def copy(t, k):
        dst = (r0, r1)[k]
        return pltpu.make_async_copy(y_ref.at[pl.ds(dest_ref[2 * (base + t) + k], 1)], dst.at[pl.ds(t, 1)], sem)

    def start(t, c):
        copy(t, 0).start()
        copy(t, 1).start()
        return c

    def wait(t, c):
        copy(t, 0).wait()
        copy(t, 1).wait()
        return c

    lax.fori_loop(0, COMB_TM, start, 0)
    lax.fori_loop(0, COMB_TM, wait, 0)
    rw = rw_ref[...]
    x2 = x_ref[...] + (rw[:, 0:1] * r0[...] + rw[:, 1:2] * r1[...])
    if final:
        ms = jnp.mean(x2 * x2, axis=-1, keepdims=True)
        x2 = x2 * lax.rsqrt(ms + RMS_EPS) * ln_ref[...]
    o_ref[...] = x2


def _combine(dest_flat, x1, rw, ln_f, ybuf, final):
    S = x1.shape[0]
    assert S % COMB_TM == 0
    grid_spec = pltpu.PrefetchScalarGridSpec(
        num_scalar_prefetch=1,
        grid=(S // COMB_TM,),
        in_specs=[
            pl.BlockSpec((COMB_TM, D_MODEL), lambda i, d: (i, 0)),
            pl.BlockSpec((COMB_TM, 128), lambda i, d: (i, 0)),
            pl.BlockSpec((1, D_MODEL), lambda i, d: (0, 0)),
            pl.BlockSpec(memory_space=pl.ANY),
        ],
        out_specs=pl.BlockSpec((COMB_TM, D_MODEL), lambda i, d: (i, 0)),
        scratch_shapes=[pltpu.VMEM((COMB_TM, D_MODEL), F32), pltpu.VMEM((COMB_TM, D_MODEL), F32),
                        pltpu.SemaphoreType.DMA(())],
    )
    return pl.pallas_call(
        functools.partial(_combine_kernel, final),
        grid_spec=grid_spec,
        out_shape=jax.ShapeDtypeStruct((S, D_MODEL), F32),
        compiler_params=_cparams(1),
        name="moe_combine",
    )(dest_flat, x1, rw, ln_f, ybuf)


def _moe_plan(ri, counts):
    S = ri.shape[0]
    cnt = counts[0, ROUTER_LANE0:ROUTER_LANE0 + N_EXPERTS].astype(I32)
    padded = (cnt + MOE_CHUNK - 1) // MOE_CHUNK * MOE_CHUNK
    pend = jnp.cumsum(padded)
    off = pend - padded
    eid = ri[:, RT_E1:RT_E2 + 1]
    rank = ri[:, RT_R1:RT_R2 + 1]
    dest = (off[eid] + rank).reshape(-1)
    n_chunks = (S * 2) // MOE_CHUNK + N_EXPERTS
    chunk_e = jnp.minimum(jnp.searchsorted(pend, jnp.arange(n_chunks, dtype=I32) * MOE_CHUNK, side='right'),
                          N_EXPERTS - 1).astype(I32)
    n_used = (pend[-1:] // MOE_CHUNK).astype(I32)
    return dest, chunk_e, n_used, n_chunks


def _attention_block(x, l, ln1, w_in, diff_lambda, diff_subln, fox_bf, cmp_pos_k, cmp_w1_k, cmp_w2_k,
                     cmp_pos_v, cmp_w1_v, cmp_w2_v, bias_a, bias_c, overlap):
    S = x.shape[0]
    w_main = jnp.concatenate([w_in[:, :B_FOX_F], w_in[:, B_FOX_F + FOX_HEADS:B_NSA_G]], axis=1).astype(BF16)
    w_gate = jnp.concatenate([w_in[:, B_FOX_F:B_FOX_F + FOX_HEADS], w_in[:, B_NSA_G:]], axis=1)
    w_gate = jnp.pad(w_gate, ((0, 0), (0, GATE_COLS - w_gate.shape[1]))).astype(BF16)
    p64, gate = _inproj(x, ln1.reshape(1, D_MODEL), w_main, w_gate)

    v128 = jnp.transpose(p64[OFF_AV:OFF_AV + 2 * DIFF_HEADS].reshape(DIFF_HEADS, 2, S, HEAD_DIM),
                         (0, 2, 1, 3)).reshape(DIFF_HEADS, S, 2 * HEAD_DIM)
    lam_init = 0.8 - 0.6 * math.exp(-0.3 * l)
    cst = jnp.zeros((1, 128), F32).at[0, 0].set(lam_init)
    o_a = _diff_attention(p64, v128, bias_a, diff_lambda.astype(F32), diff_subln.reshape(1, -1), cst)

    bf_row = jnp.pad(fox_bf.astype(F32), (0, GATE_COLS - FOX_HEADS)).reshape(1, GATE_COLS)
    c, ct = _fox_cumsum(gate, bf_row)
    o_b = _fox_attention(p64, c, ct)

    half = CMP_STRIDE * HEAD_DIM
    chunks = p64[OFF_CKC:OFF_CKC + 2 * NSA_GROUPS].reshape(2 * NSA_GROUPS, S // CMP_STRIDE, half)
    pos = jnp.stack([cmp_pos_k, cmp_pos_v]).astype(F32).reshape(2, 2, half)
    w1 = jnp.stack([cmp_w1_k, cmp_w1_v]).astype(BF16)
    w2 = jnp.stack([cmp_w2_k, cmp_w2_v]).astype(BF16)
    kvc = _compress(chunks, pos, w1, w2)
    o_c, q_aug, k_aug = _cmp_attention(p64, kvc, overlap)
    o_s = _sel_attention(q_aug, k_aug, p64, bias_c)
    o_w = _win_attention(p64, bias_c)
    return o_a, o_b, o_c, o_s, o_w, gate


def _layer(x, l, final, ln1, w_in, diff_lambda, diff_subln, fox_bf, cmp_pos_k, cmp_w1_k, cmp_w2_k,
           cmp_pos_v, cmp_w1_v, cmp_w2_v, bias_a, bias_c, overlap, w_out, ln2, wg, bg, we, be,
           w_gate, w_up, w_down, ln_f):
    S = x.shape[0]
    o_a, o_b, o_c, o_s, o_w, gate = _attention_block(
        x, l, ln1, w_in, diff_lambda, diff_subln, fox_bf, cmp_pos_k, cmp_w1_k, cmp_w2_k,
        cmp_pos_v, cmp_w1_v, cmp_w2_v, bias_a, bias_c, overlap)
    wr = jnp.pad(jnp.concatenate([wg, we], axis=1).astype(F32), ((0, 0), (0, 128 - N_GROUPS - N_EXPERTS)))
    br = jnp.pad(jnp.concatenate([bg, be]).astype(F32), (0, 128 - N_GROUPS - N_EXPERTS)).reshape(1, 128)
    x1, h2, ri, rw, counts = _outproj(o_a, o_b, o_c, o_s, o_w, gate, x, w_out.astype(BF16),
                                      ln2.reshape(1, D_MODEL), wr, br)
    dest, chunk_e, n_used, n_chunks = _moe_plan(ri, counts)
    xbuf = _dispatch(dest, h2, jnp.zeros((n_chunks * MOE_CHUNK, D_MODEL), F32))
    ybuf = _experts(chunk_e, n_used, xbuf, w_gate, w_up, w_down)
    return _combine(dest, x1, rw, ln_f.reshape(1, D_MODEL), ybuf, final)


def kernel(x, ln1, w_in, diff_lambda, diff_subln, fox_bf, cmp_pos_k, cmp_w1_k, cmp_w2_k, cmp_pos_v, cmp_w1_v,
           cmp_w2_v, t5_table, w_out, ln2, router_group_w, router_group_b, router_expert_w, router_expert_b,
           w_gate, w_up, w_down, ln_f):
    B, S, _ = x.shape
    assert B == 1
    depth = w_in.shape[0]
    bias_a = _bias_tiles(t5_table[:, :DIFF_HEADS], min(A_T, S))
    bias_c = _bias_tiles(t5_table[:, DIFF_HEADS:], min(C_T, S))
    overlap = _overlap_matrix(S // CMP_STRIDE)
    xs = x.reshape(S, D_MODEL)
    for l in range(depth):
        xs = _layer(xs, l, l == depth - 1, ln1[l], w_in[l], diff_lambda[l], diff_subln[l], fox_bf[l],
                    cmp_pos_k[l], cmp_w1_k[l], cmp_w2_k[l], cmp_pos_v[l], cmp_w1_v[l], cmp_w2_v[l],
                    bias_a, bias_c, overlap, w_out[l], ln2[l], router_group_w[l], router_group_b[l],
                    router_expert_w[l], router_expert_b[l], w_gate[l], w_up[l], w_down[l], ln_f)
    return xs.reshape(B, S, D_MODEL)
```

```python
import functools
import math

import numpy as np
import jax
import jax.numpy as jnp
from jax import lax
from jax.experimental import pallas as pl
from jax.experimental.pallas import tpu as pltpu

F32 = jnp.float32
BF16 = jnp.bfloat16
I32 = jnp.int32

D_MODEL = 2048
HEAD_DIM = 64
DIFF_HEADS = 4
FOX_HEADS = 8
NSA_HEADS = 16
NSA_GROUPS = 4
NSA_HPG = 4
CMP_BLOCK = 32
CMP_STRIDE = 16
CMP_HIDDEN = 256
SEL_BLOCK = 64
SEL_TOPK = 16
WINDOW = 512
NUM_BUCKETS = 32
MAX_DISTANCE = 128
N_GROUPS = 8
EXPERTS_PER_GROUP = 8
N_EXPERTS = 64
D_EXPERT = 256
MOE_CHUNK = 128
RMS_EPS = 1e-6
NEG_INF = -1e30
SEL_FORCE = 1e4
SCALE = HEAD_DIM ** -0.5

N_MAIN = 5632
B_FOX_F = 3072
B_NSA_G = 5640
GATE_COLS = 128
SEL_LANES = 128
SEL_MASKED = -32768.0
PICKED = -3.0e38

OFF_AQ, OFF_AK, OFF_AV = 0, 8, 16
OFF_BQ, OFF_BK, OFF_BV = 24, 32, 40
OFF_CQ = 48
OFF_CKC, OFF_CVC, OFF_CKS, OFF_CVS, OFF_CKW, OFF_CVW = 64, 68, 72, 76, 80, 84

VMEM_LIMIT = 56 * 1024 * 1024


def _cparams(n_axes):
    return pltpu.CompilerParams(dimension_semantics=("arbitrary",) * n_axes,
                                vmem_limit_bytes=VMEM_LIMIT)


IN_TM = 1024
IN_TN = 512


def _inproj_kernel(x_ref, g_ref, w_ref, wg_ref, o64_ref, og_ref, h_ref):
    j = pl.program_id(1)

    @pl.when(j == 0)
    def _():
        x = x_ref[...]
        ms = jnp.mean(x * x, axis=-1, keepdims=True)
        h_ref[...] = (x * lax.rsqrt(ms + RMS_EPS) * g_ref[...]).astype(BF16)

    res = jnp.dot(h_ref[...], w_ref[...], preferred_element_type=F32)
    for b in range(IN_TN // HEAD_DIM):
        o64_ref[b] = res[:, b * HEAD_DIM:(b + 1) * HEAD_DIM].astype(BF16)

    @pl.when(j == pl.num_programs(1) - 1)
    def _():
        og_ref[...] = jnp.dot(h_ref[...], wg_ref[...], preferred_element_type=F32)


def _inproj(x, g, w_main, w_gate):
    S = x.shape[0]
    tm = min(IN_TM, S)
    nj = N_MAIN // IN_TN
    return pl.pallas_call(
        _inproj_kernel,
        grid=(S // tm, nj),
        in_specs=[
            pl.BlockSpec((tm, D_MODEL), lambda i, j: (i, 0)),
            pl.BlockSpec((1, D_MODEL), lambda i, j: (0, 0)),
            pl.BlockSpec((D_MODEL, IN_TN), lambda i, j: (0, j)),
            pl.BlockSpec((D_MODEL, GATE_COLS), lambda i, j: (0, 0)),
        ],
        out_specs=[
            pl.BlockSpec((IN_TN // HEAD_DIM, tm, HEAD_DIM), lambda i, j: (j, i, 0)),
            pl.BlockSpec((tm, GATE_COLS), lambda i, j: (i, 0)),
        ],
        out_shape=[
            jax.ShapeDtypeStruct((N_MAIN // HEAD_DIM, S, HEAD_DIM), BF16),
            jax.ShapeDtypeStruct((S, GATE_COLS), F32),
        ],
        scratch_shapes=[pltpu.VMEM((tm, D_MODEL), BF16)],
        compiler_params=_cparams(2),
        name="inproj",
    )(x, g, w_main, w_gate)


CUM_TB = 512


def _split3(x):
    hi = x.astype(BF16)
    r1 = x - hi.astype(F32)
    mid = r1.astype(BF16)
    lo = (r1 - mid.astype(F32)).astype(BF16)
    return hi, mid, lo


def _cumsum_kernel(gate_ref, bf_ref, c_ref, ct_ref, carry_ref):
    i = pl.program_id(0)

    @pl.when(i == 0)
    def _():
        carry_ref[...] = jnp.zeros_like(carry_ref)

    z = gate_ref[...] + bf_ref[...]
    log_f = -(jnp.maximum(-z, 0.0) + jnp.log1p(jnp.exp(-jnp.abs(z))))
    tb = log_f.shape[0]
    row = lax.broadcasted_iota(I32, (tb, tb), 0)
    col = lax.broadcasted_iota(I32, (tb, tb), 1)
    tri = jnp.where(row >= col, 1.0, 0.0).astype(BF16)
    hi, mid, lo = _split3(log_f)
    c = (jnp.dot(tri, hi, preferred_element_type=F32)
         + jnp.dot(tri, mid, preferred_element_type=F32)
         + jnp.dot(tri, lo, preferred_element_type=F32)) + carry_ref[...]
    c_ref[...] = c
    ct_ref[...] = c.T[0:FOX_HEADS, :]
    carry_ref[...] = c[tb - 1:tb, :]


def _fox_cumsum(gate, bf_row):
    S = gate.shape[0]
    tb = min(CUM_TB, S)
    return pl.pallas_call(
        _cumsum_kernel,
        grid=(S // tb,),
        in_specs=[pl.BlockSpec((tb, GATE_COLS), lambda i: (i, 0)),
                  pl.BlockSpec((1, GATE_COLS), lambda i: (0, 0))],
        out_specs=[pl.BlockSpec((tb, GATE_COLS), lambda i: (i, 0)),
                   pl.BlockSpec((FOX_HEADS, tb), lambda i: (0, i))],
        out_shape=[jax.ShapeDtypeStruct((S, GATE_COLS), F32),
                   jax.ShapeDtypeStruct((FOX_HEADS, S), F32)],
        scratch_shapes=[pltpu.VMEM((1, GATE_COLS), F32)],
        compiler_params=_cparams(1),
        name="fox_cumsum",
    )(gate, bf_row)


def _tri_pairs(nq):
    qi = np.concatenate([np.full(i + 1, i, np.int32) for i in range(nq)])
    ki = np.concatenate([np.arange(i + 1, dtype=np.int32) for i in range(nq)])
    return jnp.asarray(qi), jnp.asarray(ki)


FLASH_RC = 32
STAT_LANES = 128


def _flash_block(scores, v, row0, nrows, s_ref, p_ref, alpha_ref, m_ref, l_ref, acc_ref,
                 bias_chunk=None, rc=FLASH_RC):
    tk = scores.shape[1]
    reps = tk // STAT_LANES
    s_ref[row0:row0 + nrows, :] = scores

    def chunk(c):
        sl = slice(row0 + c * rc, row0 + (c + 1) * rc)
        s = s_ref[sl, :]
        return sl, (s if bias_chunk is None else s + bias_chunk(c))

    for c in range(nrows // rc):
        sl, s = chunk(c)
        m_prev = m_ref[sl, :]
        m_new = jnp.maximum(m_prev, jnp.max(s, axis=1, keepdims=True))
        alpha_ref[sl, :] = jnp.exp(m_prev - m_new)
        m_ref[sl, :] = m_new
    for c in range(nrows // rc):
        sl, s = chunk(c)
        p = jnp.exp(s - jnp.tile(m_ref[sl, :], (1, reps)))
        part = p[:, 0:STAT_LANES]
        for r in range(1, reps):
            part = part + p[:, r * STAT_LANES:(r + 1) * STAT_LANES]
        l_ref[sl, :] = alpha_ref[sl, :] * l_ref[sl, :] + part
        p_ref[sl, :] = p.astype(BF16)
    rows = slice(row0, row0 + nrows)
    width = acc_ref.shape[1]
    acc_ref[rows, :] = (alpha_ref[rows, 0:width] * acc_ref[rows, :]
                        + jnp.dot(p_ref[rows, :], v, preferred_element_type=F32))


def _row_sum(l):
    return jnp.broadcast_to(jnp.sum(l, axis=1, keepdims=True), l.shape)


def _init_state(m_ref, l_ref, acc_ref):
    m_ref[...] = jnp.full_like(m_ref, NEG_INF)
    l_ref[...] = jnp.zeros_like(l_ref)
    acc_ref[...] = jnp.zeros_like(acc_ref)


def _qk(q, k):
    return lax.dot_general(q, k, (((1,), (1,)), ((), ())), preferred_element_type=F32)


def _t5_bucket(rel):
    n = jnp.maximum(rel, 0)
    max_exact = NUM_BUCKETS // 2
    nf = jnp.maximum(n, max_exact).astype(F32)
    large = max_exact + (jnp.log(nf / max_exact) / math.log(MAX_DISTANCE / max_exact)
                         * (NUM_BUCKETS - max_exact)).astype(I32)
    return jnp.where(n < max_exact, n, jnp.minimum(large, NUM_BUCKETS - 1))


def _bias_tiles(table, T):
    assert T + 1 >= MAX_DISTANCE
    i = jnp.arange(T)[:, None]
    j = jnp.arange(T)[None, :]
    rel = jnp.stack([i - j, T + i - j])
    t = table.astype(F32)
    bucket = _t5_bucket(rel)[..., None]
    b = jnp.zeros(rel.shape + (t.shape[1],), F32)
    for k in range(NUM_BUCKETS - 1):
        b = jnp.where(bucket == k, t[k] - t[NUM_BUCKETS - 1], b)
    b = jnp.where((rel >= 0)[..., None], b, NEG_INF)
    return jnp.transpose(b, (3, 0, 1, 2))


A_T = 512


def _diff_kernel(qt_ref, kt_ref, q1_ref, q2_ref, k1_ref, k2_ref, v_ref, bias_ref, dl_ref, sub_ref,
                 cst_ref, o_ref, qs, s_ref, p_ref, alpha, m, l, acc):
    p = pl.program_id(1)
    qi = qt_ref[p]
    ki = kt_ref[p]
    T = q1_ref.shape[1]
    half = T // 2

    @pl.when(ki == 0)
    def _():
        _init_state(m, l, acc)
        qs[0:T, :] = q1_ref[0] * SCALE
        qs[T:2 * T, :] = q2_ref[0] * SCALE

    def step(with_bias):
        for j, k_ref in enumerate((k1_ref, k2_ref)):
            for hh in range(2):
                r0 = j * T + hh * half
                bias = ((lambda c, hh=hh: bias_ref[0, 0, hh * half + c * FLASH_RC:hh * half + (c + 1) * FLASH_RC, :])
                        if with_bias else None)
                _flash_block(_qk(qs[r0:r0 + half, :], k_ref[0]), v_ref[0], r0, half,
                             s_ref, p_ref, alpha, m, l, acc, bias)

    @pl.when(qi - ki <= 1)
    def _():
        step(True)

    @pl.when(qi - ki > 1)
    def _():
        step(False)

    @pl.when(ki == qi)
    def _():
        dl = dl_ref[...]
        lam_init = cst_ref[0:1, 0:1]
        lam = (jnp.exp(jnp.sum(dl[0:1] * dl[1:2], axis=1, keepdims=True))
               - jnp.exp(jnp.sum(dl[2:3] * dl[3:4], axis=1, keepdims=True)) + lam_init)
        o = acc[0:T, :] / _row_sum(l[0:T, :]) - lam * (acc[T:2 * T, :] / _row_sum(l[T:2 * T, :]))
        ms = jnp.mean(o * o, axis=-1, keepdims=True)
        y = o * lax.rsqrt(ms + RMS_EPS) * sub_ref[...]
        o_ref[0] = (y * (1.0 - lam_init)).astype(BF16)


def _diff_attention(p64, v128, bias, dl, subln, cst):
    S = p64.shape[1]
    T = min(A_T, S)
    nq = S // T
    qt, kt = _tri_pairs(nq)
    grid_spec = pltpu.PrefetchScalarGridSpec(
        num_scalar_prefetch=2,
        grid=(DIFF_HEADS, int(qt.shape[0])),
        in_specs=[
            pl.BlockSpec((1, T, HEAD_DIM), lambda h, p, qt, kt: (OFF_AQ + 2 * h, qt[p], 0)),
            pl.BlockSpec((1, T, HEAD_DIM), lambda h, p, qt, kt: (OFF_AQ + 2 * h + 1, qt[p], 0)),
            pl.BlockSpec((1, T, HEAD_DIM), lambda h, p, qt, kt: (OFF_AK + 2 * h, kt[p], 0)),
            pl.BlockSpec((1, T, HEAD_DIM), lambda h, p, qt, kt: (OFF_AK + 2 * h + 1, kt[p], 0)),
            pl.BlockSpec((1, T, 2 * HEAD_DIM), lambda h, p, qt, kt: (h, kt[p], 0)),
            pl.BlockSpec((1, 1, T, T), lambda h, p, qt, kt: (h, jnp.minimum(qt[p] - kt[p], 1), 0, 0)),
            pl.BlockSpec((4, HEAD_DIM), lambda h, p, qt, kt: (0, 0)),
            pl.BlockSpec((1, 2 * HEAD_DIM), lambda h, p, qt, kt: (0, 0)),
            pl.BlockSpec((1, 128), lambda h, p, qt, kt: (0, 0)),
        ],
        out_specs=pl.BlockSpec((1, T, 2 * HEAD_DIM), lambda h, p, qt, kt: (h, qt[p], 0)),
        scratch_shapes=[
            pltpu.VMEM((2 * T, HEAD_DIM), BF16),
            pltpu.VMEM((2 * T, T), F32), pltpu.VMEM((2 * T, T), BF16), pltpu.VMEM((2 * T, STAT_LANES), F32),
            pltpu.VMEM((2 * T, STAT_LANES), F32), pltpu.VMEM((2 * T, STAT_LANES), F32),
            pltpu.VMEM((2 * T, 2 * HEAD_DIM), F32),
        ],
    )
    return pl.pallas_call(
        _diff_kernel,
        grid_spec=grid_spec,
        out_shape=jax.ShapeDtypeStruct((DIFF_HEADS, S, 2 * HEAD_DIM), BF16),
        compiler_params=_cparams(2),
        name="diff_attn",
    )(qt, kt, p64, p64, p64, p64, v128, bias, dl, subln, cst)


B_T = 512


def _fox_kernel(qt_ref, kt_ref, q_ref, k_ref, v_ref, c_ref, ct_ref, o_ref, qs, cq, s_ref, p_ref, alpha, m, l, acc):
    h = pl.program_id(0)
    p = pl.program_id(1)
    qi = qt_ref[p]
    ki = kt_ref[p]
    T = q_ref.shape[1]
    half = T // 2
    reps = T // STAT_LANES

    @pl.when(ki == 0)
    def _():
        _init_state(m, l, acc)
        qs[...] = q_ref[0] * SCALE
        c = c_ref[...]
        lane = lax.broadcasted_iota(I32, c.shape, 1)
        cq[...] = jnp.broadcast_to(jnp.sum(jnp.where(lane == h, c, 0.0), axis=1, keepdims=True), cq.shape)

    def step(diag):
        ck = ct_ref[pl.ds(h, 1), :]
        for hh in range(2):
            r0 = hh * half

            def decay(c, r0=r0):
                lo = r0 + c * FLASH_RC
                d = jnp.tile(cq[lo:lo + FLASH_RC, :], (1, reps)) - ck
                if diag:
                    row = lo + lax.broadcasted_iota(I32, d.shape, 0)
                    col = lax.broadcasted_iota(I32, d.shape, 1)
                    d = jnp.where(row >= col, d, NEG_INF)
                return d

            _flash_block(_qk(qs[r0:r0 + half, :], k_ref[0]), v_ref[0], r0, half,
                         s_ref, p_ref, alpha, m, l, acc, decay)

    @pl.when(ki == qi)
    def _():
        step(True)

    @pl.when(ki < qi)
    def _():
        step(False)

    @pl.when(ki == qi)
    def _():
        o_ref[0] = (acc[...] / _row_sum(l[...])[:, 0:HEAD_DIM]).astype(BF16)


def _fox_attention(p64, c, ct):
    S = p64.shape[1]
    T = min(B_T, S)
    nq = S // T
    qt, kt = _tri_pairs(nq)
    grid_spec = pltpu.PrefetchScalarGridSpec(
        num_scalar_prefetch=2,
        grid=(FOX_HEADS, int(qt.shape[0])),
        in_specs=[
            pl.BlockSpec((1, T, HEAD_DIM), lambda h, p, qt, kt: (OFF_BQ + h, qt[p], 0)),
            pl.BlockSpec((1, T, HEAD_DIM), lambda h, p, qt, kt: (OFF_BK + h, kt[p], 0)),
            pl.BlockSpec((1, T, HEAD_DIM), lambda h, p, qt, kt: (OFF_BV + h, kt[p], 0)),
            pl.BlockSpec((T, GATE_COLS), lambda h, p, qt, kt: (qt[p], 0)),
            pl.BlockSpec((FOX_HEADS, T), lambda h, p, qt, kt: (0, kt[p])),
        ],
        out_specs=pl.BlockSpec((1, T, HEAD_DIM), lambda h, p, qt, kt: (h, qt[p], 0)),
        scratch_shapes=[
            pltpu.VMEM((T, HEAD_DIM), BF16), pltpu.VMEM((T, STAT_LANES), F32),
            pltpu.VMEM((T, T), F32), pltpu.VMEM((T, T), BF16), pltpu.VMEM((T, STAT_LANES), F32),
            pltpu.VMEM((T, STAT_LANES), F32), pltpu.VMEM((T, STAT_LANES), F32), pltpu.VMEM((T, HEAD_DIM), F32),
        ],
    )
    return pl.pallas_call(
        _fox_kernel,
        grid_spec=grid_spec,
        out_shape=jax.ShapeDtypeStruct((FOX_HEADS, S, HEAD_DIM), BF16),
        compiler_params=_cparams(2),
        name="fox_attn",
    )(qt, kt, p64, p64, p64, c, ct)


def _gelu_tanh(x):
    return 0.5 * x * (1.0 + jnp.tanh(math.sqrt(2.0 / math.pi) * (x + 0.044715 * (x * x * x))))


def _compress_kernel(x_ref, pos_ref, w1_ref, w2_ref, o_ref):
    x = x_ref[0].astype(F32)
    half = x.shape[1]
    pos = pos_ref[0]
    x_lo = (x + pos[0:1]).astype(BF16)
    x_hi = (x + pos[1:2]).astype(BF16)
    w1 = w1_ref[0]
    y_lo = jnp.dot(x_lo, w1[:half], preferred_element_type=F32)
    y_hi = jnp.dot(x_hi, w1[half:], preferred_element_type=F32)
    n = y_hi.shape[0]
    pre = y_lo + pltpu.roll(y_hi, n - 1, 0)
    o_ref[0] = jnp.dot(_gelu_tanh(pre).astype(BF16), w2_ref[0], preferred_element_type=F32).astype(BF16)


def _compress(chunks, pos, w1, w2):
    n = chunks.shape[1]
    half = CMP_STRIDE * HEAD_DIM
    return pl.pallas_call(
        _compress_kernel,
        grid=(2 * NSA_GROUPS,),
        in_specs=[
            pl.BlockSpec((1, n, half), lambda i: (i, 0, 0)),
            pl.BlockSpec((1, 2, half), lambda i: (i // NSA_GROUPS, 0, 0)),
            pl.BlockSpec((1, 2 * half, CMP_HIDDEN), lambda i: (i // NSA_GROUPS, 0, 0)),
            pl.BlockSpec((1, CMP_HIDDEN, HEAD_DIM), lambda i: (i // NSA_GROUPS, 0, 0)),
        ],
        out_specs=pl.BlockSpec((1, n, HEAD_DIM), lambda i: (i, 0, 0)),
        out_shape=jax.ShapeDtypeStruct((2 * NSA_GROUPS, n, HEAD_DIM), BF16),
        compiler_params=_cparams(1),
        name="nsa_compress",
    )(chunks, pos, w1, w2)


C_T = 256


def _cmp_kernel(q_ref, kc_ref, vc_ref, ks_ref, ov_ref, oc_ref, qa_ref, ka_ref):
    qi = pl.program_id(1)
    T = q_ref.shape[1]
    ncp = kc_ref.shape[1]
    q0 = qi * T
    qs = (q_ref[...] * SCALE).reshape(NSA_HPG * T, HEAD_DIM)
    s = _qk(qs, kc_ref[0]).reshape(NSA_HPG, T, ncp)
    t = q0 + lax.broadcasted_iota(I32, (T, ncp), 0)
    n_idx = lax.broadcasted_iota(I32, (T, ncp), 1)
    visible = (CMP_STRIDE * n_idx + (CMP_BLOCK - 1) <= t)[None]
    s = jnp.where(visible, s, NEG_INF)
    smax = jnp.max(s, axis=2, keepdims=True)
    e = jnp.where(visible, jnp.exp(s - smax), 0.0)
    den = jnp.sum(e, axis=2, keepdims=True)
    p = e / jnp.where(den > 0.0, den, 1.0)
    o = jnp.dot(p.reshape(NSA_HPG * T, ncp).astype(BF16), vc_ref[0], preferred_element_type=F32)
    oc_ref[...] = o.reshape(NSA_HPG, T, HEAD_DIM).astype(BF16)

    psum = p[0] + p[1] + p[2] + p[3]
    hi, mid, lo = _split3(psum)
    ov = ov_ref[...]
    imp = (jnp.dot(hi, ov, preferred_element_type=F32) + jnp.dot(mid, ov, preferred_element_type=F32)
           + jnp.dot(lo, ov, preferred_element_type=F32))
    tq = q0 + lax.broadcasted_iota(I32, (T, SEL_LANES), 0)
    jb = lax.broadcasted_iota(I32, (T, SEL_LANES), 1)
    cur = tq // SEL_BLOCK
    forced = (jb == 0) | (jb == cur) | (jb == cur - 1)
    valid = SEL_BLOCK * jb <= tq
    score = jnp.where(valid, imp + jnp.where(forced, SEL_FORCE, 0.0), NEG_INF)
    sc = score.T
    jrow = lax.broadcasted_iota(I32, sc.shape, 0).astype(F32)
    sel = jnp.zeros(sc.shape, F32)
    for _ in range(SEL_TOPK):
        mx = jnp.max(sc, axis=0, keepdims=True)
        first = jnp.min(jnp.where(sc == mx, jrow, float(SEL_LANES)), axis=0, keepdims=True)
        pick = jrow == first
        sel = jnp.where(pick, 1.0, sel)
        sc = jnp.where(pick, PICKED, sc)
    sel_neg = jnp.where(sel.T > 0.5, 0.0, SEL_MASKED).astype(BF16)

    zeros = jnp.zeros((T, HEAD_DIM), BF16)
    for hh in range(NSA_HPG):
        qa_ref[hh, :, 0:HEAD_DIM] = qs[hh * T:(hh + 1) * T]
        qa_ref[hh, :, HEAD_DIM:2 * HEAD_DIM] = zeros
        qa_ref[hh, :, 2 * HEAD_DIM:] = sel_neg
    kblk = (q0 + lax.broadcasted_iota(I32, (T, SEL_LANES), 0)) // SEL_BLOCK
    onehot = jnp.where(kblk == jb, 1.0, 0.0).astype(BF16)
    ka_ref[0, :, 0:HEAD_DIM] = ks_ref[0]
    ka_ref[0, :, HEAD_DIM:2 * HEAD_DIM] = zeros
    ka_ref[0, :, 2 * HEAD_DIM:] = onehot


def _cmp_attention(p64, kvc, overlap):
    S = p64.shape[1]
    T = min(C_T, S)
    ncp = kvc.shape[1]
    aug = 2 * HEAD_DIM + SEL_LANES
    return pl.pallas_call(
        _cmp_kernel,
        grid=(NSA_GROUPS, S // T),
        in_specs=[
            pl.BlockSpec((NSA_HPG, T, HEAD_DIM), lambda g, i: (OFF_CQ // NSA_HPG + g, i, 0)),
            pl.BlockSpec((1, ncp, HEAD_DIM), lambda g, i: (g, 0, 0)),
            pl.BlockSpec((1, ncp, HEAD_DIM), lambda g, i: (NSA_GROUPS + g, 0, 0)),
            pl.BlockSpec((1, T, HEAD_DIM), lambda g, i: (OFF_CKS + g, i, 0)),
            pl.BlockSpec((ncp, SEL_LANES), lambda g, i: (0, 0)),
        ],
        out_specs=[
            pl.BlockSpec((NSA_HPG, T, HEAD_DIM), lambda g, i: (g, i, 0)),
            pl.BlockSpec((NSA_HPG, T, aug), lambda g, i: (g, i, 0)),
            pl.BlockSpec((1, T, aug), lambda g, i: (g, i, 0)),
        ],
        out_shape=[
            jax.ShapeDtypeStruct((NSA_HEADS, S, HEAD_DIM), BF16),
            jax.ShapeDtypeStruct((NSA_HEADS, S, aug), BF16),
            jax.ShapeDtypeStruct((NSA_GROUPS, S, aug), BF16),
        ],
        compiler_params=_cparams(2),
        name="nsa_cmp_attn",
    )(p64, kvc, kvc, p64, overlap)


def _overlap_matrix(ncp):
    n = np.arange(ncp)[:, None] * CMP_STRIDE
    j = np.arange(SEL_LANES)[None, :] * SEL_BLOCK
    ov = np.clip(np.minimum(n + CMP_BLOCK, j + SEL_BLOCK) - np.maximum(n, j), 0, None) / CMP_BLOCK
    ov[ncp - 1:] = 0.0
    return jnp.asarray(ov, BF16)


def _sel_kernel(qt_ref, kt_ref, qa_ref, ka_ref, v_ref, bias_ref, o_ref, s_ref, p_ref, alpha, m, l, acc):
    p = pl.program_id(1)
    qi = qt_ref[p]
    ki = kt_ref[p]
    T = qa_ref.shape[1]

    @pl.when(ki == 0)
    def _():
        _init_state(m, l, acc)

    def step(with_bias):
        for h in range(NSA_HPG):
            bias = (lambda c, h=h: bias_ref[h, 0, c * FLASH_RC:(c + 1) * FLASH_RC, :]) if with_bias else None
            _flash_block(_qk(qa_ref[h], ka_ref[0]), v_ref[0], h * T, T, s_ref, p_ref, alpha, m, l, acc, bias)

    @pl.when(qi - ki <= 1)
    def _():
        step(True)

    @pl.when(qi - ki > 1)
    def _():
        step(False)

    @pl.when(ki == qi)
    def _():
        o_ref[...] = (acc[...] / _row_sum(l[...])[:, :HEAD_DIM]).reshape(NSA_HPG, T, HEAD_DIM).astype(BF16)


def _sel_attention(q_aug, k_aug, p64, bias):
    S = p64.shape[1]
    T = min(C_T, S)
    aug = q_aug.shape[2]
    qt, kt = _tri_pairs(S // T)
    grid_spec = pltpu.PrefetchScalarGridSpec(
        num_scalar_prefetch=2,
        grid=(NSA_GROUPS, int(qt.shape[0])),
        in_specs=[
            pl.BlockSpec((NSA_HPG, T, aug), lambda g, p, qt, kt: (g, qt[p], 0)),
            pl.BlockSpec((1, T, aug), lambda g, p, qt, kt: (g, kt[p], 0)),
            pl.BlockSpec((1, T, HEAD_DIM), lambda g, p, qt, kt: (OFF_CVS + g, kt[p], 0)),
            pl.BlockSpec((NSA_HPG, 1, T, T), lambda g, p, qt, kt: (g, jnp.minimum(qt[p] - kt[p], 1), 0, 0)),
        ],
        out_specs=pl.BlockSpec((NSA_HPG, T, HEAD_DIM), lambda g, p, qt, kt: (g, qt[p], 0)),
        scratch_shapes=[pltpu.VMEM((NSA_HPG * T, T), F32), pltpu.VMEM((NSA_HPG * T, T), BF16),
                        pltpu.VMEM((NSA_HPG * T, STAT_LANES), F32),
                        pltpu.VMEM((NSA_HPG * T, STAT_LANES), F32), pltpu.VMEM((NSA_HPG * T, STAT_LANES), F32),
                        pltpu.VMEM((NSA_HPG * T, HEAD_DIM), F32)],
    )
    return pl.pallas_call(
        _sel_kernel,
        grid_spec=grid_spec,
        out_shape=jax.ShapeDtypeStruct((NSA_HEADS, S, HEAD_DIM), BF16),
        compiler_params=_cparams(2),
        name="nsa_sel_attn",
    )(qt, kt, q_aug, k_aug, p64, bias)


def _win_kernel(q_ref, k_ref, v_ref, bias_ref, o_ref, qs, s_ref, p_ref, alpha, m, l, acc):
    qi = pl.program_id(1)
    kk = pl.program_id(2)
    T = q_ref.shape[1]

    @pl.when(kk == 0)
    def _():
        _init_state(m, l, acc)
        qs[...] = (q_ref[...] * SCALE).reshape(NSA_HPG * T, HEAD_DIM)

    def step(kind):
        for h in range(NSA_HPG):
            if kind < 2:
                bias = lambda c, h=h: bias_ref[h, 0, c * FLASH_RC:(c + 1) * FLASH_RC, :]
            else:
                def bias(c):
                    row = c * FLASH_RC + lax.broadcasted_iota(I32, (FLASH_RC, T), 0)
                    col = lax.broadcasted_iota(I32, (FLASH_RC, T), 1)
                    return jnp.where(col > row, 0.0, NEG_INF)
            _flash_block(_qk(qs[h * T:(h + 1) * T, :], k_ref[0]), v_ref[0], h * T, T,
                         s_ref, p_ref, alpha, m, l, acc, bias)

    @pl.when(kk == 0)
    def _():
        step(0)

    @pl.when((kk == 1) & (qi >= 1))
    def _():
        step(1)

    @pl.when((kk == 2) & (qi >= 2))
    def _():
        step(2)

    @pl.when(kk == 2)
    def _():
        o_ref[...] = (acc[...] / _row_sum(l[...])[:, 0:HEAD_DIM]).reshape(NSA_HPG, T, HEAD_DIM).astype(BF16)


def _win_attention(p64, bias):
    S = p64.shape[1]
    T = min(C_T, S)
    assert WINDOW == 2 * T
    return pl.pallas_call(
        _win_kernel,
        grid=(NSA_GROUPS, S // T, 3),
        in_specs=[
            pl.BlockSpec((NSA_HPG, T, HEAD_DIM), lambda g, i, kk: (OFF_CQ // NSA_HPG + g, i, 0)),
            pl.BlockSpec((1, T, HEAD_DIM), lambda g, i, kk: (OFF_CKW + g, jnp.maximum(i - kk, 0), 0)),
            pl.BlockSpec((1, T, HEAD_DIM), lambda g, i, kk: (OFF_CVW + g, jnp.maximum(i - kk, 0), 0)),
            pl.BlockSpec((NSA_HPG, 1, T, T), lambda g, i, kk: (g, jnp.minimum(kk, 1), 0, 0)),
        ],
        out_specs=pl.BlockSpec((NSA_HPG, T, HEAD_DIM), lambda g, i, kk: (g, i, 0)),
        out_shape=jax.ShapeDtypeStruct((NSA_HEADS, S, HEAD_DIM), BF16),
        scratch_shapes=[pltpu.VMEM((NSA_HPG * T, HEAD_DIM), BF16),
                        pltpu.VMEM((NSA_HPG * T, T), F32), pltpu.VMEM((NSA_HPG * T, T), BF16),
                        pltpu.VMEM((NSA_HPG * T, STAT_LANES), F32),
                        pltpu.VMEM((NSA_HPG * T, STAT_LANES), F32), pltpu.VMEM((NSA_HPG * T, STAT_LANES), F32),
                        pltpu.VMEM((NSA_HPG * T, HEAD_DIM), F32)],
        compiler_params=_cparams(3),
        name="nsa_win_attn",
    )(p64, p64, p64, bias)


OUT_TM = 256
RT_E1, RT_E2, RT_R1, RT_R2 = 0, 1, 2, 3
ROUTER_LANE0 = N_GROUPS


def _outproj_kernel(oa_ref, ob_ref, oc_ref, os_ref, ow_ref, gate_ref, x_ref, w_ref, ln_ref, wr_ref, br_ref,
                    x1_ref, h2_ref, ri_ref, rw_ref, cnt_ref, mix_ref, carry_ref):
    i = pl.program_id(0)
    tm = x_ref.shape[0]

    @pl.when(i == 0)
    def _():
        carry_ref[...] = jnp.zeros_like(carry_ref)

    for h in range(DIFF_HEADS):
        mix_ref[:, 128 * h:128 * (h + 1)] = oa_ref[h]
    for h in range(FOX_HEADS):
        c0 = 512 + HEAD_DIM * h
        mix_ref[:, c0:c0 + HEAD_DIM] = ob_ref[h]
    sig = jax.nn.sigmoid(gate_ref[...])
    for h in range(NSA_HEADS):
        c0 = 1024 + HEAD_DIM * h
        g0 = sig[:, FOX_HEADS + h:FOX_HEADS + h + 1]
        g1 = sig[:, FOX_HEADS + NSA_HEADS + h:FOX_HEADS + NSA_HEADS + h + 1]
        g2 = sig[:, FOX_HEADS + 2 * NSA_HEADS + h:FOX_HEADS + 2 * NSA_HEADS + h + 1]
        o = g0 * oc_ref[h].astype(F32) + g1 * os_ref[h].astype(F32) + g2 * ow_ref[h].astype(F32)
        mix_ref[:, c0:c0 + HEAD_DIM] = o.astype(BF16)

    x1 = x_ref[...] + jnp.dot(mix_ref[...], w_ref[...], preferred_element_type=F32)
    x1_ref[...] = x1
    ms = jnp.mean(x1 * x1, axis=-1, keepdims=True)
    h2 = x1 * lax.rsqrt(ms + RMS_EPS) * ln_ref[...]
    h2_ref[...] = h2

    a0, a1, a2 = _split3(h2)
    b0, b1, b2 = _split3(wr_ref[...])
    dot = lambda a, b: jnp.dot(a, b, preferred_element_type=F32)
    logits = (dot(a0, b0) + (dot(a0, b1) + dot(a1, b0))
              + (dot(a0, b2) + dot(a1, b1) + dot(a2, b0))) + br_ref[...]
    lane = lax.broadcasted_iota(I32, logits.shape, 1)
    lane_f = lane.astype(F32)

    def first_lane(cond):
        return jnp.min(jnp.where(cond, lane_f, 128.0), axis=1, keepdims=True).astype(I32)

    is_g = lane < N_GROUPS
    lg = jnp.where(is_g, logits, NEG_INF)
    gmax = jnp.max(lg, axis=1, keepdims=True)
    gsum = jnp.sum(jnp.where(is_g, jnp.exp(lg - gmax), 0.0), axis=1, keepdims=True)
    g_p = 1.0 / gsum
    g_idx = first_lane(lg == gmax)
    e_lane = lane - ROUTER_LANE0
    in_group = (e_lane >= 0) & (e_lane < N_EXPERTS) & ((e_lane >> 3) == g_idx)
    em = jnp.where(in_group, logits, NEG_INF)
    v1 = jnp.max(em, axis=1, keepdims=True)
    i1 = first_lane(em == v1)
    em2 = jnp.where(lane == i1, PICKED, em)
    v2 = jnp.max(em2, axis=1, keepdims=True)
    i2 = first_lane(em2 == v2)
    tt = jnp.exp(v2 - v1)
    w1 = g_p / (1.0 + tt)
    w2 = g_p * tt / (1.0 + tt)

    oh = jnp.where((lane == i1) | (lane == i2), 1.0, 0.0)
    row = lax.broadcasted_iota(I32, (tm, tm), 0)
    col = lax.broadcasted_iota(I32, (tm, tm), 1)
    tri = jnp.where(row > col, 1.0, 0.0).astype(BF16)
    prefix = jnp.dot(tri, oh.astype(BF16), preferred_element_type=F32) + carry_ref[...]
    r1 = jnp.sum(jnp.where(lane == i1, prefix, 0.0), axis=1, keepdims=True)
    r2 = jnp.sum(jnp.where(lane == i2, prefix, 0.0), axis=1, keepdims=True)
    carry = carry_ref[...] + jnp.sum(oh, axis=0, keepdims=True)
    carry_ref[...] = carry
    cnt_ref[...] = carry

    ri = jnp.where(lane == RT_E1, i1 - ROUTER_LANE0,
                   jnp.where(lane == RT_E2, i2 - ROUTER_LANE0,
                             jnp.where(lane == RT_R1, r1.astype(I32),
                                       jnp.where(lane == RT_R2, r2.astype(I32), 0))))
    ri_ref[...] = ri
    rw_ref[...] = jnp.where(lane == 0, w1, jnp.where(lane == 1, w2, 0.0))


def _outproj(o_a, o_b, o_c, o_s, o_w, gate, x, w_out, ln2, wr, br):
    S = x.shape[0]
    tm = min(OUT_TM, S)
    full = lambda shape: pl.BlockSpec(shape, lambda i: (0,) * len(shape))
    return pl.pallas_call(
        _outproj_kernel,
        grid=(S // tm,),
        in_specs=[
            pl.BlockSpec((DIFF_HEADS, tm, 2 * HEAD_DIM), lambda i: (0, i, 0)),
            pl.BlockSpec((FOX_HEADS, tm, HEAD_DIM), lambda i: (0, i, 0)),
            pl.BlockSpec((NSA_HEADS, tm, HEAD_DIM), lambda i: (0, i, 0)),
            pl.BlockSpec((NSA_HEADS, tm, HEAD_DIM), lambda i: (0, i, 0)),
            pl.BlockSpec((NSA_HEADS, tm, HEAD_DIM), lambda i: (0, i, 0)),
            pl.BlockSpec((tm, GATE_COLS), lambda i: (i, 0)),
            pl.BlockSpec((tm, D_MODEL), lambda i: (i, 0)),
            full((D_MODEL, D_MODEL)),
            full((1, D_MODEL)),
            full((D_MODEL, 128)),
            full((1, 128)),
        ],
        out_specs=[
            pl.BlockSpec((tm, D_MODEL), lambda i: (i, 0)),
            pl.BlockSpec((tm, D_MODEL), lambda i: (i, 0)),
            pl.BlockSpec((tm, 128), lambda i: (i, 0)),
            pl.BlockSpec((tm, 128), lambda i: (i, 0)),
            pl.BlockSpec((1, 128), lambda i: (0, 0)),
        ],
        out_shape=[
            jax.ShapeDtypeStruct((S, D_MODEL), F32),
            jax.ShapeDtypeStruct((S, D_MODEL), F32),
            jax.ShapeDtypeStruct((S, 128), I32),
            jax.ShapeDtypeStruct((S, 128), F32),
            jax.ShapeDtypeStruct((1, 128), F32),
        ],
        scratch_shapes=[pltpu.VMEM((tm, D_MODEL), BF16), pltpu.VMEM((1, 128), F32)],
        compiler_params=_cparams(1),
        name="outproj_router",
    )(o_a, o_b, o_c, o_s, o_w, gate, x, w_out, ln2, wr, br)


DISP_TM = 256


def _dispatch_kernel(dest_ref, h_ref, xb_in_ref, xb_ref, sem):
    del xb_in_ref
    i = pl.program_id(0)
    base = i * DISP_TM

    def copy(t, k):
        return pltpu.make_async_copy(h_ref.at[pl.ds(t, 1)],
                                     xb_ref.at[pl.ds(dest_ref[2 * (base + t) + k], 1)], sem)

    def start(t, c):
        copy(t, 0).start()
        copy(t, 1).start()
        return c

    def wait(t, c):
        copy(t, 0).wait()
        copy(t, 1).wait()
        return c

    lax.fori_loop(0, DISP_TM, start, 0)
    lax.fori_loop(0, DISP_TM, wait, 0)


def _dispatch(dest_flat, h2, xbuf0):
    S = h2.shape[0]
    assert S % DISP_TM == 0
    grid_spec = pltpu.PrefetchScalarGridSpec(
        num_scalar_prefetch=1,
        grid=(S // DISP_TM,),
        in_specs=[pl.BlockSpec((DISP_TM, D_MODEL), lambda i, d: (i, 0)), pl.BlockSpec(memory_space=pl.ANY)],
        out_specs=pl.BlockSpec(memory_space=pl.ANY),
        scratch_shapes=[pltpu.SemaphoreType.DMA(())],
    )
    return pl.pallas_call(
        _dispatch_kernel,
        grid_spec=grid_spec,
        out_shape=jax.ShapeDtypeStruct(xbuf0.shape, xbuf0.dtype),
        input_output_aliases={2: 0},
        compiler_params=pltpu.CompilerParams(dimension_semantics=("arbitrary",), has_side_effects=True),
        name="moe_dispatch",
    )(dest_flat, h2, xbuf0)


def _expert_kernel(ce_ref, used_ref, x_ref, wg_ref, wu_ref, wd_ref, y_ref):
    c = pl.program_id(0)

    @pl.when(c < used_ref[0])
    def _():
        x = x_ref[...].astype(BF16)
        g = jnp.dot(x, wg_ref[0].astype(BF16), preferred_element_type=F32)
        u = jnp.dot(x, wu_ref[0].astype(BF16), preferred_element_type=F32)
        hmid = (g * jax.nn.sigmoid(g) * u).astype(BF16)
        y_ref[...] = jnp.dot(hmid, wd_ref[0].astype(BF16), preferred_element_type=F32)

    @pl.when(c >= used_ref[0])
    def _():
        y_ref[...] = jnp.zeros_like(y_ref)


def _experts(chunk_e, n_used, xbuf, w_gate, w_up, w_down):
    P = xbuf.shape[0]
    n_chunks = P // MOE_CHUNK
    row_blk = lambda c, ce, nu: (jnp.minimum(c, nu[0] - 1), 0)
    wt_blk = lambda c, ce, nu: (ce[jnp.minimum(c, nu[0] - 1)], 0, 0)
    grid_spec = pltpu.PrefetchScalarGridSpec(
        num_scalar_prefetch=2,
        grid=(n_chunks,),
        in_specs=[
            pl.BlockSpec((MOE_CHUNK, D_MODEL), row_blk),
            pl.BlockSpec((1, D_MODEL, D_EXPERT), wt_blk),
            pl.BlockSpec((1, D_MODEL, D_EXPERT), wt_blk),
            pl.BlockSpec((1, D_EXPERT, D_MODEL), wt_blk),
        ],
        out_specs=pl.BlockSpec((MOE_CHUNK, D_MODEL), lambda c, ce, nu: (c, 0)),
    )
    return pl.pallas_call(
        _expert_kernel,
        grid_spec=grid_spec,
        out_shape=jax.ShapeDtypeStruct((P, D_MODEL), F32),
        compiler_params=_cparams(1),
        name="moe_experts",
    )(chunk_e, n_used, xbuf, w_gate, w_up, w_down)


COMB_TM = 256


def _combine_kernel(final, dest_ref, x_ref, rw_ref, ln_ref, y_ref, o_ref, r0, r1, sem):
    i = pl.program_id(0)
    base = i * COMB_TM

    def copy(t, k):
        dst = (r0, r1)[k]
        return pltpu.make_async_copy(y_ref.at[pl.ds(dest_ref[2 * (base + t) + k], 1)], dst.at[pl.ds(t, 1)], sem)

    def start(t, c):
        copy(t, 0).start()
        copy(t, 1).start()
        return c

    def wait(t, c):
        copy(t, 0).wait()
        copy(t, 1).wait()
        return c

    lax.fori_loop(0, COMB_TM, start, 0)
    lax.fori_loop(0, COMB_TM, wait, 0)
    rw = rw_ref[...]
    x2 = x_ref[...] + (rw[:, 0:1] * r0[...] + rw[:, 1:2] * r1[...])
    if final:
        ms = jnp.mean(x2 * x2, axis=-1, keepdims=True)
        x2 = x2 * lax.rsqrt(ms + RMS_EPS) * ln_ref[...]
    o_ref[...] = x2


def _combine(dest_flat, x1, rw, ln_f, ybuf, final):
    S = x1.shape[0]
    assert S % COMB_TM == 0
    grid_spec = pltpu.PrefetchScalarGridSpec(
        num_scalar_prefetch=1,
        grid=(S // COMB_TM,),
        in_specs=[
            pl.BlockSpec((COMB_TM, D_MODEL), lambda i, d: (i, 0)),
            pl.BlockSpec((COMB_TM, 128), lambda i, d: (i, 0)),
            pl.BlockSpec((1, D_MODEL), lambda i, d: (0, 0)),
            pl.BlockSpec(memory_space=pl.ANY),
        ],
        out_specs=pl.BlockSpec((COMB_TM, D_MODEL), lambda i, d: (i, 0)),
        scratch_shapes=[pltpu.VMEM((COMB_TM, D_MODEL), F32), pltpu.VMEM((COMB_TM, D_MODEL), F32),
                        pltpu.SemaphoreType.DMA(())],
    )
    return pl.pallas_call(
        functools.partial(_combine_kernel, final),
        grid_spec=grid_spec,
        out_shape=jax.ShapeDtypeStruct((S, D_MODEL), F32),
        compiler_params=_cparams(1),
        name="moe_combine",
    )(dest_flat, x1, rw, ln_f, ybuf)


def _moe_plan(ri, counts):
    S = ri.shape[0]
    cnt = counts[0, ROUTER_LANE0:ROUTER_LANE0 + N_EXPERTS].astype(I32)
    padded = (cnt + MOE_CHUNK - 1) // MOE_CHUNK * MOE_CHUNK
    pend = jnp.cumsum(padded)
    off = pend - padded
    eid = ri[:, RT_E1:RT_E2 + 1]
    rank = ri[:, RT_R1:RT_R2 + 1]
    dest = (off[eid] + rank).reshape(-1)
    n_chunks = (S * 2) // MOE_CHUNK + N_EXPERTS
    chunk_e = jnp.minimum(jnp.searchsorted(pend, jnp.arange(n_chunks, dtype=I32) * MOE_CHUNK, side='right'),
                          N_EXPERTS - 1).astype(I32)
    n_used = (pend[-1:] // MOE_CHUNK).astype(I32)
    return dest, chunk_e, n_used, n_chunks


def _attention_block(x, l, ln1, w_in, diff_lambda, diff_subln, fox_bf, cmp_pos_k, cmp_w1_k, cmp_w2_k,
                     cmp_pos_v, cmp_w1_v, cmp_w2_v, bias_a, bias_c, overlap):
    S = x.shape[0]
    w_main = jnp.concatenate([w_in[:, :B_FOX_F], w_in[:, B_FOX_F + FOX_HEADS:B_NSA_G]], axis=1).astype(BF16)
    w_gate = jnp.concatenate([w_in[:, B_FOX_F:B_FOX_F + FOX_HEADS], w_in[:, B_NSA_G:]], axis=1)
    w_gate = jnp.pad(w_gate, ((0, 0), (0, GATE_COLS - w_gate.shape[1]))).astype(BF16)
    p64, gate = _inproj(x, ln1.reshape(1, D_MODEL), w_main, w_gate)

    v128 = jnp.transpose(p64[OFF_AV:OFF_AV + 2 * DIFF_HEADS].reshape(DIFF_HEADS, 2, S, HEAD_DIM),
                         (0, 2, 1, 3)).reshape(DIFF_HEADS, S, 2 * HEAD_DIM)
    lam_init = 0.8 - 0.6 * math.exp(-0.3 * l)
    cst = jnp.zeros((1, 128), F32).at[0, 0].set(lam_init)
    o_a = _diff_attention(p64, v128, bias_a, diff_lambda.astype(F32), diff_subln.reshape(1, -1), cst)

    bf_row = jnp.pad(fox_bf.astype(F32), (0, GATE_COLS - FOX_HEADS)).reshape(1, GATE_COLS)
    c, ct = _fox_cumsum(gate, bf_row)
    o_b = _fox_attention(p64, c, ct)

    half = CMP_STRIDE * HEAD_DIM
    chunks = p64[OFF_CKC:OFF_CKC + 2 * NSA_GROUPS].reshape(2 * NSA_GROUPS, S // CMP_STRIDE, half)
    pos = jnp.stack([cmp_pos_k, cmp_pos_v]).astype(F32).reshape(2, 2, half)
    w1 = jnp.stack([cmp_w1_k, cmp_w1_v]).astype(BF16)
    w2 = jnp.stack([cmp_w2_k, cmp_w2_v]).astype(BF16)
    kvc = _compress(chunks, pos, w1, w2)
    o_c, q_aug, k_aug = _cmp_attention(p64, kvc, overlap)
    o_s = _sel_attention(q_aug, k_aug, p64, bias_c)
    o_w = _win_attention(p64, bias_c)
    return o_a, o_b, o_c, o_s, o_w, gate


def _layer(x, l, final, ln1, w_in, diff_lambda, diff_subln, fox_bf, cmp_pos_k, cmp_w1_k, cmp_w2_k,
           cmp_pos_v, cmp_w1_v, cmp_w2_v, bias_a, bias_c, overlap, w_out, ln2, wg, bg, we, be,
           w_gate, w_up, w_down, ln_f):
    S = x.shape[0]
    o_a, o_b, o_c, o_s, o_w, gate = _attention_block(
        x, l, ln1, w_in, diff_lambda, diff_subln, fox_bf, cmp_pos_k, cmp_w1_k, cmp_w2_k,
        cmp_pos_v, cmp_w1_v, cmp_w2_v, bias_a, bias_c, overlap)
    wr = jnp.pad(jnp.concatenate([wg, we], axis=1).astype(F32), ((0, 0), (0, 128 - N_GROUPS - N_EXPERTS)))
    br = jnp.pad(jnp.concatenate([bg, be]).astype(F32), (0, 128 - N_GROUPS - N_EXPERTS)).reshape(1, 128)
    x1, h2, ri, rw, counts = _outproj(o_a, o_b, o_c, o_s, o_w, gate, x, w_out.astype(BF16),
                                      ln2.reshape(1, D_MODEL), wr, br)
    dest, chunk_e, n_used, n_chunks = _moe_plan(ri, counts)
    xbuf = _dispatch(dest, h2, jnp.zeros((n_chunks * MOE_CHUNK, D_MODEL), F32))
    ybuf = _experts(chunk_e + l * N_EXPERTS, n_used, xbuf, w_gate, w_up, w_down)
    return _combine(dest, x1, rw, ln_f.reshape(1, D_MODEL), ybuf, final)


def kernel(x, ln1, w_in, diff_lambda, diff_subln, fox_bf, cmp_pos_k, cmp_w1_k, cmp_w2_k, cmp_pos_v, cmp_w1_v,
           cmp_w2_v, t5_table, w_out, ln2, router_group_w, router_group_b, router_expert_w, router_expert_b,
           w_gate, w_up, w_down, ln_f):
    B, S, _ = x.shape
    assert B == 1
    depth = w_in.shape[0]
    bias_a = _bias_tiles(t5_table[:, :DIFF_HEADS], min(A_T, S))
    bias_c = _bias_tiles(t5_table[:, DIFF_HEADS:], min(C_T, S))
    overlap = _overlap_matrix(S // CMP_STRIDE)
    xs = x.reshape(S, D_MODEL)
    wg_all = w_gate.reshape(depth * N_EXPERTS, D_MODEL, D_EXPERT)
    wu_all = w_up.reshape(depth * N_EXPERTS, D_MODEL, D_EXPERT)
    wd_all = w_down.reshape(depth * N_EXPERTS, D_EXPERT, D_MODEL)
    for l in range(depth):
        xs = _layer(xs, l, l == depth - 1, ln1[l], w_in[l], diff_lambda[l], diff_subln[l], fox_bf[l],
                    cmp_pos_k[l], cmp_w1_k[l], cmp_w2_k[l], cmp_pos_v[l], cmp_w1_v[l], cmp_w2_v[l],
                    bias_a, bias_c, overlap, w_out[l], ln2[l], router_group_w[l], router_group_b[l],
                    router_expert_w[l], router_expert_b[l], wg_all, wu_all, wd_all, ln_f)
    return xs.reshape(B, S, D_MODEL)
```

```python
import functools
import math

import numpy as np
import jax
import jax.numpy as jnp
from jax import lax
from jax.experimental import pallas as pl
from jax.experimental.pallas import tpu as pltpu

F32 = jnp.float32
BF16 = jnp.bfloat16
I32 = jnp.int32

D_MODEL = 2048
HEAD_DIM = 64
DIFF_HEADS = 4
FOX_HEADS = 8
NSA_HEADS = 16
NSA_GROUPS = 4
NSA_HPG = 4
CMP_BLOCK = 32
CMP_STRIDE = 16
CMP_HIDDEN = 256
SEL_BLOCK = 64
SEL_TOPK = 16
WINDOW = 512
NUM_BUCKETS = 32
MAX_DISTANCE = 128
N_GROUPS = 8
EXPERTS_PER_GROUP = 8
N_EXPERTS = 64
D_EXPERT = 256
MOE_CHUNK = 128
RMS_EPS = 1e-6
NEG_INF = -1e30
SEL_FORCE = 1e4
SCALE = HEAD_DIM ** -0.5

N_MAIN = 5632
B_FOX_F = 3072
B_NSA_G = 5640
GATE_COLS = 128
SEL_LANES = 128
SEL_MASKED = -32768.0
PICKED = -3.0e38

OFF_AQ, OFF_AK, OFF_AV = 0, 8, 16
OFF_BQ, OFF_BK, OFF_BV = 24, 32, 40
OFF_CQ = 48
OFF_CKC, OFF_CVC, OFF_CKS, OFF_CVS, OFF_CKW, OFF_CVW = 64, 68, 72, 76, 80, 84

VMEM_LIMIT = 56 * 1024 * 1024


def _cparams(n_axes):
    return pltpu.CompilerParams(dimension_semantics=("arbitrary",) * n_axes,
                                vmem_limit_bytes=VMEM_LIMIT)


IN_TM = 1024
IN_TN = 512


def _inproj_kernel(x_ref, g_ref, w_ref, wg_ref, o64_ref, og_ref, h_ref):
    j = pl.program_id(1)

    @pl.when(j == 0)
    def _():
        x = x_ref[...]
        ms = jnp.mean(x * x, axis=-1, keepdims=True)
        h_ref[...] = (x * lax.rsqrt(ms + RMS_EPS) * g_ref[...]).astype(BF16)

    res = jnp.dot(h_ref[...], w_ref[...], preferred_element_type=F32)
    for b in range(IN_TN // HEAD_DIM):
        o64_ref[b] = res[:, b * HEAD_DIM:(b + 1) * HEAD_DIM].astype(BF16)

    @pl.when(j == pl.num_programs(1) - 1)
    def _():
        og_ref[...] = jnp.dot(h_ref[...], wg_ref[...], preferred_element_type=F32)


def _inproj(x, g, w_main, w_gate):
    S = x.shape[0]
    tm = min(IN_TM, S)
    nj = N_MAIN // IN_TN
    return pl.pallas_call(
        _inproj_kernel,
        grid=(S // tm, nj),
        in_specs=[
            pl.BlockSpec((tm, D_MODEL), lambda i, j: (i, 0)),
            pl.BlockSpec((1, D_MODEL), lambda i, j: (0, 0)),
            pl.BlockSpec((D_MODEL, IN_TN), lambda i, j: (0, j)),
            pl.BlockSpec((D_MODEL, GATE_COLS), lambda i, j: (0, 0)),
        ],
        out_specs=[
            pl.BlockSpec((IN_TN // HEAD_DIM, tm, HEAD_DIM), lambda i, j: (j, i, 0)),
            pl.BlockSpec((tm, GATE_COLS), lambda i, j: (i, 0)),
        ],
        out_shape=[
            jax.ShapeDtypeStruct((N_MAIN // HEAD_DIM, S, HEAD_DIM), BF16),
            jax.ShapeDtypeStruct((S, GATE_COLS), F32),
        ],
        scratch_shapes=[pltpu.VMEM((tm, D_MODEL), BF16)],
        compiler_params=_cparams(2),
        name="inproj",
    )(x, g, w_main, w_gate)


CUM_TB = 512


def _split3(x):
    hi = x.astype(BF16)
    r1 = x - hi.astype(F32)
    mid = r1.astype(BF16)
    lo = (r1 - mid.astype(F32)).astype(BF16)
    return hi, mid, lo


def _cumsum_kernel(gate_ref, bf_ref, c_ref, ct_ref, carry_ref):
    i = pl.program_id(0)

    @pl.when(i == 0)
    def _():
        carry_ref[...] = jnp.zeros_like(carry_ref)

    z = gate_ref[...] + bf_ref[...]
    log_f = -(jnp.maximum(-z, 0.0) + jnp.log1p(jnp.exp(-jnp.abs(z))))
    tb = log_f.shape[0]
    row = lax.broadcasted_iota(I32, (tb, tb), 0)
    col = lax.broadcasted_iota(I32, (tb, tb), 1)
    tri = jnp.where(row >= col, 1.0, 0.0).astype(BF16)
    hi, mid, lo = _split3(log_f)
    c = (jnp.dot(tri, hi, preferred_element_type=F32)
         + jnp.dot(tri, mid, preferred_element_type=F32)
         + jnp.dot(tri, lo, preferred_element_type=F32)) + carry_ref[...]
    c2 = c * LOG2E
    c_ref[...] = c2
    ct_ref[...] = c2.T[0:FOX_HEADS, :]
    carry_ref[...] = c[tb - 1:tb, :]


def _fox_cumsum(gate, bf_row):
    S = gate.shape[0]
    tb = min(CUM_TB, S)
    return pl.pallas_call(
        _cumsum_kernel,
        grid=(S // tb,),
        in_specs=[pl.BlockSpec((tb, GATE_COLS), lambda i: (i, 0)),
                  pl.BlockSpec((1, GATE_COLS), lambda i: (0, 0))],
        out_specs=[pl.BlockSpec((tb, GATE_COLS), lambda i: (i, 0)),
                   pl.BlockSpec((FOX_HEADS, tb), lambda i: (0, i))],
        out_shape=[jax.ShapeDtypeStruct((S, GATE_COLS), F32),
                   jax.ShapeDtypeStruct((FOX_HEADS, S), F32)],
        scratch_shapes=[pltpu.VMEM((1, GATE_COLS), F32)],
        compiler_params=_cparams(1),
        name="fox_cumsum",
    )(gate, bf_row)


def _tile_pairs(S, tq, tk):
    assert tk % tq == 0
    qi_l, ki_l, cls_l, last_l = [], [], [], []
    for qi in range(S // tq):
        n_k = (qi * tq + tq - 1) // tk + 1
        for ki in range(n_k):
            qi_l.append(qi)
            ki_l.append(ki)
            cls_l.append((qi * tq - ki * tk) // tq)
            last_l.append(int(ki == n_k - 1))
    return tuple(jnp.asarray(np.asarray(a, np.int32)) for a in (qi_l, ki_l, cls_l, last_l))


def _n_bias_classes(tq, tk):
    return -(-(tk + MAX_DISTANCE) // tq)


STAT_LANES = 128
LOG2E = 1.4426950408889634


def _chunk_rows(tk):
    return max(16, min(32, 16384 // tk))


def _flash_block(scores, v, row0, nrows, s_ref, p_ref, alpha_ref, m_ref, l_ref, acc_ref, addend=None):
    tk = scores.shape[1]
    reps = tk // STAT_LANES
    rc = _chunk_rows(tk)
    s_ref[row0:row0 + nrows, :] = scores

    def chunk(c):
        sl = slice(row0 + c * rc, row0 + (c + 1) * rc)
        s = s_ref[sl, :]
        return sl, (s if addend is None else s + addend(c * rc, rc))

    for c in range(nrows // rc):
        sl, s = chunk(c)
        m_prev = m_ref[sl, :]
        m_new = jnp.maximum(m_prev, jnp.max(s, axis=1, keepdims=True))
        alpha_ref[sl, :] = jnp.exp2(m_prev - m_new)
        m_ref[sl, :] = m_new
    for c in range(nrows // rc):
        sl, s = chunk(c)
        p = jnp.exp2(s - jnp.tile(m_ref[sl, :], (1, reps)))
        part = p[:, 0:STAT_LANES]
        for r in range(1, reps):
            part = part + p[:, r * STAT_LANES:(r + 1) * STAT_LANES]
        l_ref[sl, :] = alpha_ref[sl, :] * l_ref[sl, :] + part
        p_ref[sl, :] = p.astype(BF16)
    rows = slice(row0, row0 + nrows)
    width = acc_ref.shape[1]
    acc_ref[rows, :] = (alpha_ref[rows, 0:width] * acc_ref[rows, :]
                        + jnp.dot(p_ref[rows, :], v, preferred_element_type=F32))


def _row_sum(l):
    return jnp.broadcast_to(jnp.sum(l, axis=1, keepdims=True), l.shape)


def _init_state(m_ref, l_ref, acc_ref):
    m_ref[...] = jnp.full_like(m_ref, NEG_INF)
    l_ref[...] = jnp.zeros_like(l_ref)
    acc_ref[...] = jnp.zeros_like(acc_ref)


def _qk(q, k):
    return lax.dot_general(q, k, (((1,), (1,)), ((), ())), preferred_element_type=F32)


def _t5_bucket(rel):
    n = jnp.maximum(rel, 0)
    max_exact = NUM_BUCKETS // 2
    nf = jnp.maximum(n, max_exact).astype(F32)
    large = max_exact + (jnp.log(nf / max_exact) / math.log(MAX_DISTANCE / max_exact)
                         * (NUM_BUCKETS - max_exact)).astype(I32)
    return jnp.where(n < max_exact, n, jnp.minimum(large, NUM_BUCKETS - 1))


def _bias_tiles(table, tq, tk):
    n_cls = _n_bias_classes(tq, tk)
    i = jnp.arange(tq)[:, None]
    j = jnp.arange(tk)[None, :]
    rel = jnp.stack([c * tq + i - j for c in range(n_cls)])
    t = table.astype(F32)
    bucket = _t5_bucket(rel)[..., None]
    b = jnp.zeros(rel.shape + (t.shape[1],), F32)
    for k in range(NUM_BUCKETS - 1):
        b = jnp.where(bucket == k, (t[k] - t[NUM_BUCKETS - 1]) * LOG2E, b)
    b = jnp.where((rel >= 0)[..., None], b, NEG_INF)
    return jnp.transpose(b, (3, 0, 1, 2))


A_TQ = 512
A_TK = 1024


def _diff_kernel(n_cls, qt_ref, kt_ref, cls_ref, last_ref, q1_ref, q2_ref, k1_ref, k2_ref, v_ref, bias_ref,
                 dl_ref, sub_ref, cst_ref, o_ref, qs, s_ref, p_ref, alpha, m, l, acc):
    p = pl.program_id(1)
    ki = kt_ref[p]
    T = q1_ref.shape[1]
    half = T // 2

    @pl.when(ki == 0)
    def _():
        _init_state(m, l, acc)
        qs[0:T, :] = q1_ref[0] * (SCALE * LOG2E)
        qs[T:2 * T, :] = q2_ref[0] * (SCALE * LOG2E)

    def step(with_bias):
        for j, k_ref in enumerate((k1_ref, k2_ref)):
            for hh in range(2):
                r0 = j * T + hh * half
                bias = ((lambda r, n, hh=hh: bias_ref[0, 0, hh * half + r:hh * half + r + n, :])
                        if with_bias else None)
                _flash_block(_qk(qs[r0:r0 + half, :], k_ref[0]), v_ref[0], r0, half,
                             s_ref, p_ref, alpha, m, l, acc, bias)

    @pl.when(cls_ref[p] < n_cls)
    def _():
        step(True)

    @pl.when(cls_ref[p] >= n_cls)
    def _():
        step(False)

    @pl.when(last_ref[p] == 1)
    def _():
        dl = dl_ref[...]
        lam_init = cst_ref[0:1, 0:1]
        lam = (jnp.exp(jnp.sum(dl[0:1] * dl[1:2], axis=1, keepdims=True))
               - jnp.exp(jnp.sum(dl[2:3] * dl[3:4], axis=1, keepdims=True)) + lam_init)
        o = acc[0:T, :] / _row_sum(l[0:T, :]) - lam * (acc[T:2 * T, :] / _row_sum(l[T:2 * T, :]))
        ms = jnp.mean(o * o, axis=-1, keepdims=True)
        y = o * lax.rsqrt(ms + RMS_EPS) * sub_ref[...]
        o_ref[0] = (y * (1.0 - lam_init)).astype(BF16)


def _diff_attention(p64, v128, bias, dl, subln, cst):
    S = p64.shape[1]
    T, TK = bias.shape[2], bias.shape[3]
    n_cls = bias.shape[1]
    tabs = _tile_pairs(S, T, TK)
    qblk = lambda off: pl.BlockSpec((1, T, HEAD_DIM), lambda h, p, qt, kt, cl, la: (off + 2 * h, qt[p], 0))
    kblk = lambda off: pl.BlockSpec((1, TK, HEAD_DIM), lambda h, p, qt, kt, cl, la: (off + 2 * h, kt[p], 0))
    const = lambda shape: pl.BlockSpec(shape, lambda h, p, qt, kt, cl, la: (0,) * len(shape))
    grid_spec = pltpu.PrefetchScalarGridSpec(
        num_scalar_prefetch=4,
        grid=(DIFF_HEADS, int(tabs[0].shape[0])),
        in_specs=[
            qblk(OFF_AQ), qblk(OFF_AQ + 1), kblk(OFF_AK), kblk(OFF_AK + 1),
            pl.BlockSpec((1, TK, 2 * HEAD_DIM), lambda h, p, qt, kt, cl, la: (h, kt[p], 0)),
            pl.BlockSpec((1, 1, T, TK), lambda h, p, qt, kt, cl, la: (h, jnp.minimum(cl[p], n_cls - 1), 0, 0)),
            const((4, HEAD_DIM)), const((1, 2 * HEAD_DIM)), const((1, 128)),
        ],
        out_specs=pl.BlockSpec((1, T, 2 * HEAD_DIM), lambda h, p, qt, kt, cl, la: (h, qt[p], 0)),
        scratch_shapes=[
            pltpu.VMEM((2 * T, HEAD_DIM), BF16),
            pltpu.VMEM((2 * T, TK), F32), pltpu.VMEM((2 * T, TK), BF16), pltpu.VMEM((2 * T, STAT_LANES), F32),
            pltpu.VMEM((2 * T, STAT_LANES), F32), pltpu.VMEM((2 * T, STAT_LANES), F32),
            pltpu.VMEM((2 * T, 2 * HEAD_DIM), F32),
        ],
    )
    return pl.pallas_call(
        functools.partial(_diff_kernel, n_cls),
        grid_spec=grid_spec,
        out_shape=jax.ShapeDtypeStruct((DIFF_HEADS, S, 2 * HEAD_DIM), BF16),
        compiler_params=_cparams(2),
        name="diff_attn",
    )(*tabs, p64, p64, p64, p64, v128, bias, dl, subln, cst)


B_TQ = 512
B_TK = 1024


def _fox_kernel(diag_cls, qt_ref, kt_ref, cls_ref, last_ref, q_ref, k_ref, v_ref, c_ref, ct_ref, o_ref,
                qs, cq, s_ref, p_ref, alpha, m, l, acc):
    h = pl.program_id(0)
    p = pl.program_id(1)
    T = q_ref.shape[1]
    TK = k_ref.shape[1]
    half = T // 2
    reps = TK // STAT_LANES

    @pl.when(kt_ref[p] == 0)
    def _():
        _init_state(m, l, acc)
        qs[...] = q_ref[0] * (SCALE * LOG2E)
        c = c_ref[...]
        lane = lax.broadcasted_iota(I32, c.shape, 1)
        cq[...] = jnp.broadcast_to(jnp.sum(jnp.where(lane == h, c, 0.0), axis=1, keepdims=True), cq.shape)

    def step(diag):
        ck = ct_ref[pl.ds(h, 1), :]
        lead = cls_ref[p] * T
        for hh in range(2):
            r0 = hh * half

            def decay(r, n, r0=r0):
                d = jnp.tile(cq[r0 + r:r0 + r + n, :], (1, reps)) - ck
                if diag:
                    row = lead + (r0 + r) + lax.broadcasted_iota(I32, d.shape, 0)
                    col = lax.broadcasted_iota(I32, d.shape, 1)
                    d = jnp.where(row >= col, d, NEG_INF)
                return d

            _flash_block(_qk(qs[r0:r0 + half, :], k_ref[0]), v_ref[0], r0, half,
                         s_ref, p_ref, alpha, m, l, acc, decay)

    @pl.when(cls_ref[p] < diag_cls)
    def _():
        step(True)

    @pl.when(cls_ref[p] >= diag_cls)
    def _():
        step(False)

    @pl.when(last_ref[p] == 1)
    def _():
        o_ref[0] = (acc[...] / _row_sum(l[...])[:, 0:HEAD_DIM]).astype(BF16)


def _fox_attention(p64, c, ct):
    S = p64.shape[1]
    T = min(B_TQ, S)
    TK = min(B_TK, S)
    tabs = _tile_pairs(S, T, TK)
    grid_spec = pltpu.PrefetchScalarGridSpec(
        num_scalar_prefetch=4,
        grid=(FOX_HEADS, int(tabs[0].shape[0])),
        in_specs=[
            pl.BlockSpec((1, T, HEAD_DIM), lambda h, p, qt, kt, cl, la: (OFF_BQ + h, qt[p], 0)),
            pl.BlockSpec((1, TK, HEAD_DIM), lambda h, p, qt, kt, cl, la: (OFF_BK + h, kt[p], 0)),
            pl.BlockSpec((1, TK, HEAD_DIM), lambda h, p, qt, kt, cl, la: (OFF_BV + h, kt[p], 0)),
            pl.BlockSpec((T, GATE_COLS), lambda h, p, qt, kt, cl, la: (qt[p], 0)),
            pl.BlockSpec((FOX_HEADS, TK), lambda h, p, qt, kt, cl, la: (0, kt[p])),
        ],
        out_specs=pl.BlockSpec((1, T, HEAD_DIM), lambda h, p, qt, kt, cl, la: (h, qt[p], 0)),
        scratch_shapes=[
            pltpu.VMEM((T, HEAD_DIM), BF16), pltpu.VMEM((T, STAT_LANES), F32),
            pltpu.VMEM((T, TK), F32), pltpu.VMEM((T, TK), BF16), pltpu.VMEM((T, STAT_LANES), F32),
            pltpu.VMEM((T, STAT_LANES), F32), pltpu.VMEM((T, STAT_LANES), F32), pltpu.VMEM((T, HEAD_DIM), F32),
        ],
    )
    return pl.pallas_call(
        functools.partial(_fox_kernel, TK // T),
        grid_spec=grid_spec,
        out_shape=jax.ShapeDtypeStruct((FOX_HEADS, S, HEAD_DIM), BF16),
        compiler_params=_cparams(2),
        name="fox_attn",
    )(*tabs, p64, p64, p64, c, ct)


def _gelu_tanh(x):
    return 0.5 * x * (1.0 + jnp.tanh(math.sqrt(2.0 / math.pi) * (x + 0.044715 * (x * x * x))))


def _compress_kernel(x_ref, pos_ref, w1_ref, w2_ref, o_ref):
    x = x_ref[0].astype(F32)
    half = x.shape[1]
    pos = pos_ref[0]
    x_lo = (x + pos[0:1]).astype(BF16)
    x_hi = (x + pos[1:2]).astype(BF16)
    w1 = w1_ref[0]
    y_lo = jnp.dot(x_lo, w1[:half], preferred_element_type=F32)
    y_hi = jnp.dot(x_hi, w1[half:], preferred_element_type=F32)
    n = y_hi.shape[0]
    pre = y_lo + pltpu.roll(y_hi, n - 1, 0)
    o_ref[0] = jnp.dot(_gelu_tanh(pre).astype(BF16), w2_ref[0], preferred_element_type=F32).astype(BF16)


def _compress(chunks, pos, w1, w2):
    n = chunks.shape[1]
    half = CMP_STRIDE * HEAD_DIM
    return pl.pallas_call(
        _compress_kernel,
        grid=(2 * NSA_GROUPS,),
        in_specs=[
            pl.BlockSpec((1, n, half), lambda i: (i, 0, 0)),
            pl.BlockSpec((1, 2, half), lambda i: (i // NSA_GROUPS, 0, 0)),
            pl.BlockSpec((1, 2 * half, CMP_HIDDEN), lambda i: (i // NSA_GROUPS, 0, 0)),
            pl.BlockSpec((1, CMP_HIDDEN, HEAD_DIM), lambda i: (i // NSA_GROUPS, 0, 0)),
        ],
        out_specs=pl.BlockSpec((1, n, HEAD_DIM), lambda i: (i, 0, 0)),
        out_shape=jax.ShapeDtypeStruct((2 * NSA_GROUPS, n, HEAD_DIM), BF16),
        compiler_params=_cparams(1),
        name="nsa_compress",
    )(chunks, pos, w1, w2)


C_T = 256


def _cmp_kernel(q_ref, kc_ref, vc_ref, ks_ref, ov_ref, oc_ref, qa_ref, ka_ref):
    qi = pl.program_id(1)
    T = q_ref.shape[1]
    ncp = kc_ref.shape[1]
    q0 = qi * T
    qs = (q_ref[...] * SCALE).reshape(NSA_HPG * T, HEAD_DIM)
    s = _qk(qs, kc_ref[0]).reshape(NSA_HPG, T, ncp)
    t = q0 + lax.broadcasted_iota(I32, (T, ncp), 0)
    n_idx = lax.broadcasted_iota(I32, (T, ncp), 1)
    visible = (CMP_STRIDE * n_idx + (CMP_BLOCK - 1) <= t)[None]
    s = jnp.where(visible, s, NEG_INF)
    smax = jnp.max(s, axis=2, keepdims=True)
    e = jnp.where(visible, jnp.exp(s - smax), 0.0)
    den = jnp.sum(e, axis=2, keepdims=True)
    p = e / jnp.where(den > 0.0, den, 1.0)
    o = jnp.dot(p.reshape(NSA_HPG * T, ncp).astype(BF16), vc_ref[0], preferred_element_type=F32)
    oc_ref[...] = o.reshape(NSA_HPG, T, HEAD_DIM).astype(BF16)

    psum = p[0] + p[1] + p[2] + p[3]
    hi, mid, lo = _split3(psum)
    ov = ov_ref[...]
    imp = (jnp.dot(hi, ov, preferred_element_type=F32) + jnp.dot(mid, ov, preferred_element_type=F32)
           + jnp.dot(lo, ov, preferred_element_type=F32))
    tq = q0 + lax.broadcasted_iota(I32, (T, SEL_LANES), 0)
    jb = lax.broadcasted_iota(I32, (T, SEL_LANES), 1)
    cur = tq // SEL_BLOCK
    forced = (jb == 0) | (jb == cur) | (jb == cur - 1)
    valid = SEL_BLOCK * jb <= tq
    score = jnp.where(valid, imp + jnp.where(forced, SEL_FORCE, 0.0), NEG_INF)
    sc = score.T
    jrow = lax.broadcasted_iota(I32, sc.shape, 0).astype(F32)
    sel = jnp.zeros(sc.shape, F32)
    for _ in range(SEL_TOPK):
        mx = jnp.max(sc, axis=0, keepdims=True)
        first = jnp.min(jnp.where(sc == mx, jrow, float(SEL_LANES)), axis=0, keepdims=True)
        pick = jrow == first
        sel = jnp.where(pick, 1.0, sel)
        sc = jnp.where(pick, PICKED, sc)
    sel_neg = jnp.where(sel.T > 0.5, 0.0, SEL_MASKED).astype(BF16)

    zeros = jnp.zeros((T, HEAD_DIM), BF16)
    for hh in range(NSA_HPG):
        qa_ref[hh, :, 0:HEAD_DIM] = q_ref[hh] * (SCALE * LOG2E)
        qa_ref[hh, :, HEAD_DIM:2 * HEAD_DIM] = zeros
        qa_ref[hh, :, 2 * HEAD_DIM:] = sel_neg
    kblk = (q0 + lax.broadcasted_iota(I32, (T, SEL_LANES), 0)) // SEL_BLOCK
    onehot = jnp.where(kblk == jb, 1.0, 0.0).astype(BF16)
    ka_ref[0, :, 0:HEAD_DIM] = ks_ref[0]
    ka_ref[0, :, HEAD_DIM:2 * HEAD_DIM] = zeros
    ka_ref[0, :, 2 * HEAD_DIM:] = onehot


def _cmp_attention(p64, kvc, overlap):
    S = p64.shape[1]
    T = min(C_T, S)
    ncp = kvc.shape[1]
    aug = 2 * HEAD_DIM + SEL_LANES
    return pl.pallas_call(
        _cmp_kernel,
        grid=(NSA_GROUPS, S // T),
        in_specs=[
            pl.BlockSpec((NSA_HPG, T, HEAD_DIM), lambda g, i: (OFF_CQ // NSA_HPG + g, i, 0)),
            pl.BlockSpec((1, ncp, HEAD_DIM), lambda g, i: (g, 0, 0)),
            pl.BlockSpec((1, ncp, HEAD_DIM), lambda g, i: (NSA_GROUPS + g, 0, 0)),
            pl.BlockSpec((1, T, HEAD_DIM), lambda g, i: (OFF_CKS + g, i, 0)),
            pl.BlockSpec((ncp, SEL_LANES), lambda g, i: (0, 0)),
        ],
        out_specs=[
            pl.BlockSpec((NSA_HPG, T, HEAD_DIM), lambda g, i: (g, i, 0)),
            pl.BlockSpec((NSA_HPG, T, aug), lambda g, i: (g, i, 0)),
            pl.BlockSpec((1, T, aug), lambda g, i: (g, i, 0)),
        ],
        out_shape=[
            jax.ShapeDtypeStruct((NSA_HEADS, S, HEAD_DIM), BF16),
            jax.ShapeDtypeStruct((NSA_HEADS, S, aug), BF16),
            jax.ShapeDtypeStruct((NSA_GROUPS, S, aug), BF16),
        ],
        compiler_params=_cparams(2),
        name="nsa_cmp_attn",
    )(p64, kvc, kvc, p64, overlap)


def _overlap_matrix(ncp):
    n = np.arange(ncp)[:, None] * CMP_STRIDE
    j = np.arange(SEL_LANES)[None, :] * SEL_BLOCK
    ov = np.clip(np.minimum(n + CMP_BLOCK, j + SEL_BLOCK) - np.maximum(n, j), 0, None) / CMP_BLOCK
    ov[ncp - 1:] = 0.0
    return jnp.asarray(ov, BF16)


SEL_TK = 512


def _sel_kernel(n_cls, qt_ref, kt_ref, cls_ref, last_ref, qa_ref, ka_ref, v_ref, bias_ref, o_ref,
                s_ref, p_ref, alpha, m, l, acc):
    p = pl.program_id(1)
    T = qa_ref.shape[1]

    @pl.when(kt_ref[p] == 0)
    def _():
        _init_state(m, l, acc)

    def step(with_bias):
        for h in range(NSA_HPG):
            bias = (lambda r, n, h=h: bias_ref[h, 0, r:r + n, :]) if with_bias else None
            _flash_block(_qk(qa_ref[h], ka_ref[0]), v_ref[0], h * T, T, s_ref, p_ref, alpha, m, l, acc, bias)

    @pl.when(cls_ref[p] < n_cls)
    def _():
        step(True)

    @pl.when(cls_ref[p] >= n_cls)
    def _():
        step(False)

    @pl.when(last_ref[p] == 1)
    def _():
        o_ref[...] = (acc[...] / _row_sum(l[...])[:, :HEAD_DIM]).reshape(NSA_HPG, T, HEAD_DIM).astype(BF16)


def _sel_attention(q_aug, k_aug, p64, bias):
    S = p64.shape[1]
    n_cls, T, TK = bias.shape[1:]
    aug = q_aug.shape[2]
    tabs = _tile_pairs(S, T, TK)
    grid_spec = pltpu.PrefetchScalarGridSpec(
        num_scalar_prefetch=4,
        grid=(NSA_GROUPS, int(tabs[0].shape[0])),
        in_specs=[
            pl.BlockSpec((NSA_HPG, T, aug), lambda g, p, qt, kt, cl, la: (g, qt[p], 0)),
            pl.BlockSpec((1, TK, aug), lambda g, p, qt, kt, cl, la: (g, kt[p], 0)),
            pl.BlockSpec((1, TK, HEAD_DIM), lambda g, p, qt, kt, cl, la: (OFF_CVS + g, kt[p], 0)),
            pl.BlockSpec((NSA_HPG, 1, T, TK), lambda g, p, qt, kt, cl, la: (g, jnp.minimum(cl[p], n_cls - 1), 0, 0)),
        ],
        out_specs=pl.BlockSpec((NSA_HPG, T, HEAD_DIM), lambda g, p, qt, kt, cl, la: (g, qt[p], 0)),
        scratch_shapes=[pltpu.VMEM((NSA_HPG * T, TK), F32), pltpu.VMEM((NSA_HPG * T, TK), BF16),
                        pltpu.VMEM((NSA_HPG * T, STAT_LANES), F32),
                        pltpu.VMEM((NSA_HPG * T, STAT_LANES), F32), pltpu.VMEM((NSA_HPG * T, STAT_LANES), F32),
                        pltpu.VMEM((NSA_HPG * T, HEAD_DIM), F32)],
    )
    return pl.pallas_call(
        functools.partial(_sel_kernel, n_cls),
        grid_spec=grid_spec,
        out_shape=jax.ShapeDtypeStruct((NSA_HEADS, S, HEAD_DIM), BF16),
        compiler_params=_cparams(2),
        name="nsa_sel_attn",
    )(*tabs, q_aug, k_aug, p64, bias)


def _win_kernel(q_ref, k_ref, v_ref, bias_ref, o_ref, qs, s_ref, p_ref, alpha, m, l, acc):
    qi = pl.program_id(1)
    kk = pl.program_id(2)
    T = q_ref.shape[1]

    @pl.when(kk == 0)
    def _():
        _init_state(m, l, acc)
        qs[...] = (q_ref[...] * (SCALE * LOG2E)).reshape(NSA_HPG * T, HEAD_DIM)

    def step(kind):
        for h in range(NSA_HPG):
            if kind < 2:
                bias = lambda r, n, h=h: bias_ref[h, 0, r:r + n, :]
            else:
                def bias(r, n):
                    row = r + lax.broadcasted_iota(I32, (n, T), 0)
                    col = lax.broadcasted_iota(I32, (n, T), 1)
                    return jnp.where(col > row, 0.0, NEG_INF)
            _flash_block(_qk(qs[h * T:(h + 1) * T, :], k_ref[0]), v_ref[0], h * T, T,
                         s_ref, p_ref, alpha, m, l, acc, bias)

    @pl.when(kk == 0)
    def _():
        step(0)

    @pl.when((kk == 1) & (qi >= 1))
    def _():
        step(1)

    @pl.when((kk == 2) & (qi >= 2))
    def _():
        step(2)

    @pl.when(kk == 2)
    def _():
        o_ref[...] = (acc[...] / _row_sum(l[...])[:, 0:HEAD_DIM]).reshape(NSA_HPG, T, HEAD_DIM).astype(BF16)


def _win_attention(p64, bias):
    S = p64.shape[1]
    T = min(C_T, S)
    assert WINDOW == 2 * T
    return pl.pallas_call(
        _win_kernel,
        grid=(NSA_GROUPS, S // T, 3),
        in_specs=[
            pl.BlockSpec((NSA_HPG, T, HEAD_DIM), lambda g, i, kk: (OFF_CQ // NSA_HPG + g, i, 0)),
            pl.BlockSpec((1, T, HEAD_DIM), lambda g, i, kk: (OFF_CKW + g, jnp.maximum(i - kk, 0), 0)),
            pl.BlockSpec((1, T, HEAD_DIM), lambda g, i, kk: (OFF_CVW + g, jnp.maximum(i - kk, 0), 0)),
            pl.BlockSpec((NSA_HPG, 1, T, T), lambda g, i, kk: (g, jnp.minimum(kk, 1), 0, 0)),
        ],
        out_specs=pl.BlockSpec((NSA_HPG, T, HEAD_DIM), lambda g, i, kk: (g, i, 0)),
        out_shape=jax.ShapeDtypeStruct((NSA_HEADS, S, HEAD_DIM), BF16),
        scratch_shapes=[pltpu.VMEM((NSA_HPG * T, HEAD_DIM), BF16),
                        pltpu.VMEM((NSA_HPG * T, T), F32), pltpu.VMEM((NSA_HPG * T, T), BF16),
                        pltpu.VMEM((NSA_HPG * T, STAT_LANES), F32),
                        pltpu.VMEM((NSA_HPG * T, STAT_LANES), F32), pltpu.VMEM((NSA_HPG * T, STAT_LANES), F32),
                        pltpu.VMEM((NSA_HPG * T, HEAD_DIM), F32)],
        compiler_params=_cparams(3),
        name="nsa_win_attn",
    )(p64, p64, p64, bias)


OUT_TM = 256
RT_E1, RT_E2, RT_R1, RT_R2 = 0, 1, 2, 3
ROUTER_LANE0 = N_GROUPS


def _outproj_kernel(oa_ref, ob_ref, oc_ref, os_ref, ow_ref, gate_ref, x_ref, w_ref, ln_ref, wr_ref, br_ref,
                    x1_ref, h2_ref, ri_ref, rw_ref, cnt_ref, mix_ref, carry_ref):
    i = pl.program_id(0)
    tm = x_ref.shape[0]

    @pl.when(i == 0)
    def _():
        carry_ref[...] = jnp.zeros_like(carry_ref)

    for h in range(DIFF_HEADS):
        mix_ref[:, 128 * h:128 * (h + 1)] = oa_ref[h]
    for h in range(FOX_HEADS):
        c0 = 512 + HEAD_DIM * h
        mix_ref[:, c0:c0 + HEAD_DIM] = ob_ref[h]
    sig = jax.nn.sigmoid(gate_ref[...])
    for h in range(NSA_HEADS):
        c0 = 1024 + HEAD_DIM * h
        g0 = sig[:, FOX_HEADS + h:FOX_HEADS + h + 1]
        g1 = sig[:, FOX_HEADS + NSA_HEADS + h:FOX_HEADS + NSA_HEADS + h + 1]
        g2 = sig[:, FOX_HEADS + 2 * NSA_HEADS + h:FOX_HEADS + 2 * NSA_HEADS + h + 1]
        o = g0 * oc_ref[h].astype(F32) + g1 * os_ref[h].astype(F32) + g2 * ow_ref[h].astype(F32)
        mix_ref[:, c0:c0 + HEAD_DIM] = o.astype(BF16)

    x1 = x_ref[...] + jnp.dot(mix_ref[...], w_ref[...], preferred_element_type=F32)
    x1_ref[...] = x1
    ms = jnp.mean(x1 * x1, axis=-1, keepdims=True)
    h2 = x1 * lax.rsqrt(ms + RMS_EPS) * ln_ref[...]
    h2_ref[...] = h2

    a0, a1, a2 = _split3(h2)
    b0, b1, b2 = _split3(wr_ref[...])
    dot = lambda a, b: jnp.dot(a, b, preferred_element_type=F32)
    logits = (dot(a0, b0) + (dot(a0, b1) + dot(a1, b0))
              + (dot(a0, b2) + dot(a1, b1) + dot(a2, b0))) + br_ref[...]
    lane = lax.broadcasted_iota(I32, logits.shape, 1)
    lane_f = lane.astype(F32)

    def first_lane(cond):
        return jnp.min(jnp.where(cond, lane_f, 128.0), axis=1, keepdims=True).astype(I32)

    is_g = lane < N_GROUPS
    lg = jnp.where(is_g, logits, NEG_INF)
    gmax = jnp.max(lg, axis=1, keepdims=True)
    gsum = jnp.sum(jnp.where(is_g, jnp.exp(lg - gmax), 0.0), axis=1, keepdims=True)
    g_p = 1.0 / gsum
    g_idx = first_lane(lg == gmax)
    e_lane = lane - ROUTER_LANE0
    in_group = (e_lane >= 0) & (e_lane < N_EXPERTS) & ((e_lane >> 3) == g_idx)
    em = jnp.where(in_group, logits, NEG_INF)
    v1 = jnp.max(em, axis=1, keepdims=True)
    i1 = first_lane(em == v1)
    em2 = jnp.where(lane == i1, PICKED, em)
    v2 = jnp.max(em2, axis=1, keepdims=True)
    i2 = first_lane(em2 == v2)
    tt = jnp.exp(v2 - v1)
    w1 = g_p / (1.0 + tt)
    w2 = g_p * tt / (1.0 + tt)

    oh = jnp.where((lane == i1) | (lane == i2), 1.0, 0.0)
    row = lax.broadcasted_iota(I32, (tm, tm), 0)
    col = lax.broadcasted_iota(I32, (tm, tm), 1)
    tri = jnp.where(row > col, 1.0, 0.0).astype(BF16)
    prefix = jnp.dot(tri, oh.astype(BF16), preferred_element_type=F32) + carry_ref[...]
    r1 = jnp.sum(jnp.where(lane == i1, prefix, 0.0), axis=1, keepdims=True)
    r2 = jnp.sum(jnp.where(lane == i2, prefix, 0.0), axis=1, keepdims=True)
    carry = carry_ref[...] + jnp.sum(oh, axis=0, keepdims=True)
    carry_ref[...] = carry
    cnt_ref[...] = carry

    ri = jnp.where(lane == RT_E1, i1 - ROUTER_LANE0,
                   jnp.where(lane == RT_E2, i2 - ROUTER_LANE0,
                             jnp.where(lane == RT_R1, r1.astype(I32),
                                       jnp.where(lane == RT_R2, r2.astype(I32), 0))))
    ri_ref[...] = ri
    rw_ref[...] = jnp.where(lane == 0, w1, jnp.where(lane == 1, w2, 0.0))


def _outproj(o_a, o_b, o_c, o_s, o_w, gate, x, w_out, ln2, wr, br):
    S = x.shape[0]
    tm = min(OUT_TM, S)
    full = lambda shape: pl.BlockSpec(shape, lambda i: (0,) * len(shape))
    return pl.pallas_call(
        _outproj_kernel,
        grid=(S // tm,),
        in_specs=[
            pl.BlockSpec((DIFF_HEADS, tm, 2 * HEAD_DIM), lambda i: (0, i, 0)),
            pl.BlockSpec((FOX_HEADS, tm, HEAD_DIM), lambda i: (0, i, 0)),
            pl.BlockSpec((NSA_HEADS, tm, HEAD_DIM), lambda i: (0, i, 0)),
            pl.BlockSpec((NSA_HEADS, tm, HEAD_DIM), lambda i: (0, i, 0)),
            pl.BlockSpec((NSA_HEADS, tm, HEAD_DIM), lambda i: (0, i, 0)),
            pl.BlockSpec((tm, GATE_COLS), lambda i: (i, 0)),
            pl.BlockSpec((tm, D_MODEL), lambda i: (i, 0)),
            full((D_MODEL, D_MODEL)),
            full((1, D_MODEL)),
            full((D_MODEL, 128)),
            full((1, 128)),
        ],
        out_specs=[
            pl.BlockSpec((tm, D_MODEL), lambda i: (i, 0)),
            pl.BlockSpec((tm, D_MODEL), lambda i: (i, 0)),
            pl.BlockSpec((tm, 128), lambda i: (i, 0)),
            pl.BlockSpec((tm, 128), lambda i: (i, 0)),
            pl.BlockSpec((1, 128), lambda i: (0, 0)),
        ],
        out_shape=[
            jax.ShapeDtypeStruct((S, D_MODEL), F32),
            jax.ShapeDtypeStruct((S, D_MODEL), F32),
            jax.ShapeDtypeStruct((S, 128), I32),
            jax.ShapeDtypeStruct((S, 128), F32),
            jax.ShapeDtypeStruct((1, 128), F32),
        ],
        scratch_shapes=[pltpu.VMEM((tm, D_MODEL), BF16), pltpu.VMEM((1, 128), F32)],
        compiler_params=_cparams(1),
        name="outproj_router",
    )(o_a, o_b, o_c, o_s, o_w, gate, x, w_out, ln2, wr, br)


DISP_TM = 256


def _dispatch_kernel(dest_ref, h_ref, xb_in_ref, xb_ref, sem):
    del xb_in_ref
    i = pl.program_id(0)
    base = i * DISP_TM

    def copy(t, k):
        return pltpu.make_async_copy(h_ref.at[pl.ds(t, 1)],
                                     xb_ref.at[pl.ds(dest_ref[2 * (base + t) + k], 1)], sem)

    def start(t, c):
        copy(t, 0).start()
        copy(t, 1).start()
        return c

    def wait(t, c):
        copy(t, 0).wait()
        copy(t, 1).wait()
        return c

    lax.fori_loop(0, DISP_TM, start, 0)
    lax.fori_loop(0, DISP_TM, wait, 0)


def _dispatch(dest_flat, h2, xbuf0):
    S = h2.shape[0]
    assert S % DISP_TM == 0
    grid_spec = pltpu.PrefetchScalarGridSpec(
        num_scalar_prefetch=1,
        grid=(S // DISP_TM,),
        in_specs=[pl.BlockSpec((DISP_TM, D_MODEL), lambda i, d: (i, 0)), pl.BlockSpec(memory_space=pl.ANY)],
        out_specs=pl.BlockSpec(memory_space=pl.ANY),
        scratch_shapes=[pltpu.SemaphoreType.DMA(())],
    )
    return pl.pallas_call(
        _dispatch_kernel,
        grid_spec=grid_spec,
        out_shape=jax.ShapeDtypeStruct(xbuf0.shape, xbuf0.dtype),
        input_output_aliases={2: 0},
        compiler_params=pltpu.CompilerParams(dimension_semantics=("arbitrary",), has_side_effects=True),
        name="moe_dispatch",
    )(dest_flat, h2, xbuf0)


def _expert_kernel(ce_ref, used_ref, x_ref, wg_ref, wu_ref, wd_ref, y_ref):
    c = pl.program_id(0)

    @pl.when(c < used_ref[0])
    def _():
        x = x_ref[...].astype(BF16)
        g = jnp.dot(x, wg_ref[0].astype(BF16), preferred_element_type=F32)
        u = jnp.dot(x, wu_ref[0].astype(BF16), preferred_element_type=F32)
        hmid = (g * jax.nn.sigmoid(g) * u).astype(BF16)
        y_ref[...] = jnp.dot(hmid, wd_ref[0].astype(BF16), preferred_element_type=F32)

    @pl.when(c >= used_ref[0])
    def _():
        y_ref[...] = jnp.zeros_like(y_ref)


def _experts(chunk_e, n_used, xbuf, w_gate, w_up, w_down):
    P = xbuf.shape[0]
    n_chunks = P // MOE_CHUNK
    row_blk = lambda c, ce, nu: (jnp.minimum(c, nu[0] - 1), 0)
    wt_blk = lambda c, ce, nu: (ce[jnp.minimum(c, nu[0] - 1)], 0, 0)
    grid_spec = pltpu.PrefetchScalarGridSpec(
        num_scalar_prefetch=2,
        grid=(n_chunks,),
        in_specs=[
            pl.BlockSpec((MOE_CHUNK, D_MODEL), row_blk),
            pl.BlockSpec((1, D_MODEL, D_EXPERT), wt_blk),
            pl.BlockSpec((1, D_MODEL, D_EXPERT), wt_blk),
            pl.BlockSpec((1, D_EXPERT, D_MODEL), wt_blk),
        ],
        out_specs=pl.BlockSpec((MOE_CHUNK, D_MODEL), lambda c, ce, nu: (c, 0)),
    )
    return pl.pallas_call(
        _expert_kernel,
        grid_spec=grid_spec,
        out_shape=jax.ShapeDtypeStruct((P, D_MODEL), F32),
        compiler_params=_cparams(1),
        name="moe_experts",
    )(chunk_e, n_used, xbuf, w_gate, w_up, w_down)


COMB_TM = 256


def _combine_kernel(final, dest_ref, x_ref, rw_ref, ln_ref, y_ref, o_ref, r0, r1, sem):
    i = pl.program_id(0)
    base = i * COMB_TM

    def copy(t, k):
        dst = (r0, r1)[k]
        return pltpu.make_async_copy(y_ref.at[pl.ds(dest_ref[2 * (base + t) + k], 1)], dst.at[pl.ds(t, 1)], sem)

    def start(t, c):
        copy(t, 0).start()
        copy(t, 1).start()
        return c

    def wait(t, c):
        copy(t, 0).wait()
        copy(t, 1).wait()
        return c

    lax.fori_loop(0, COMB_TM, start, 0)
    lax.fori_loop(0, COMB_TM, wait, 0)
    rw = rw_ref[...]
    x2 = x_ref[...] + (rw[:, 0:1] * r0[...] + rw[:, 1:2] * r1[...])
    if final:
        ms = jnp.mean(x2 * x2, axis=-1, keepdims=True)
        x2 = x2 * lax.rsqrt(ms + RMS_EPS) * ln_ref[...]
    o_ref[...] = x2


def _combine(dest_flat, x1, rw, ln_f, ybuf, final):
    S = x1.shape[0]
    assert S % COMB_TM == 0
    grid_spec = pltpu.PrefetchScalarGridSpec(
        num_scalar_prefetch=1,
        grid=(S // COMB_TM,),
        in_specs=[
            pl.BlockSpec((COMB_TM, D_MODEL), lambda i, d: (i, 0)),
            pl.BlockSpec((COMB_TM, 128), lambda i, d: (i, 0)),
            pl.BlockSpec((1, D_MODEL), lambda i, d: (0, 0)),
            pl.BlockSpec(memory_space=pl.ANY),
        ],
        out_specs=pl.BlockSpec((COMB_TM, D_MODEL), lambda i, d: (i, 0)),
        scratch_shapes=[pltpu.VMEM((COMB_TM, D_MODEL), F32), pltpu.VMEM((COMB_TM, D_MODEL), F32),
                        pltpu.SemaphoreType.DMA(())],
    )
    return pl.pallas_call(
        functools.partial(_combine_kernel, final),
        grid_spec=grid_spec,
        out_shape=jax.ShapeDtypeStruct((S, D_MODEL), F32),
        compiler_params=_cparams(1),
        name="moe_combine",
    )(dest_flat, x1, rw, ln_f, ybuf)


def _moe_plan(ri, counts):
    S = ri.shape[0]
    cnt = counts[0, ROUTER_LANE0:ROUTER_LANE0 + N_EXPERTS].astype(I32)
    padded = (cnt + MOE_CHUNK - 1) // MOE_CHUNK * MOE_CHUNK
    pend = jnp.cumsum(padded)
    off = pend - padded
    eid = ri[:, RT_E1:RT_E2 + 1]
    rank = ri[:, RT_R1:RT_R2 + 1]
    dest = (off[eid] + rank).reshape(-1)
    n_chunks = (S * 2) // MOE_CHUNK + N_EXPERTS
    chunk_e = jnp.minimum(jnp.searchsorted(pend, jnp.arange(n_chunks, dtype=I32) * MOE_CHUNK, side='right'),
                          N_EXPERTS - 1).astype(I32)
    n_used = (pend[-1:] // MOE_CHUNK).astype(I32)
    return dest, chunk_e, n_used, n_chunks


def _attention_block(x, l, ln1, w_in, diff_lambda, diff_subln, fox_bf, cmp_pos_k, cmp_w1_k, cmp_w2_k,
                     cmp_pos_v, cmp_w1_v, cmp_w2_v, bias_a, bias_c, overlap):
    S = x.shape[0]
    w_main = jnp.concatenate([w_in[:, :B_FOX_F], w_in[:, B_FOX_F + FOX_HEADS:B_NSA_G]], axis=1).astype(BF16)
    w_gate = jnp.concatenate([w_in[:, B_FOX_F:B_FOX_F + FOX_HEADS], w_in[:, B_NSA_G:]], axis=1)
    w_gate = jnp.pad(w_gate, ((0, 0), (0, GATE_COLS - w_gate.shape[1]))).astype(BF16)
    p64, gate = _inproj(x, ln1.reshape(1, D_MODEL), w_main, w_gate)

    v128 = jnp.transpose(p64[OFF_AV:OFF_AV + 2 * DIFF_HEADS].reshape(DIFF_HEADS, 2, S, HEAD_DIM),
                         (0, 2, 1, 3)).reshape(DIFF_HEADS, S, 2 * HEAD_DIM)
    lam_init = 0.8 - 0.6 * math.exp(-0.3 * l)
    cst = jnp.zeros((1, 128), F32).at[0, 0].set(lam_init)
    o_a = _diff_attention(p64, v128, bias_a, diff_lambda.astype(F32), diff_subln.reshape(1, -1), cst)

    bf_row = jnp.pad(fox_bf.astype(F32), (0, GATE_COLS - FOX_HEADS)).reshape(1, GATE_COLS)
    c, ct = _fox_cumsum(gate, bf_row)
    o_b = _fox_attention(p64, c, ct)

    half = CMP_STRIDE * HEAD_DIM
    chunks = p64[OFF_CKC:OFF_CKC + 2 * NSA_GROUPS].reshape(2 * NSA_GROUPS, S // CMP_STRIDE, half)
    pos = jnp.stack([cmp_pos_k, cmp_pos_v]).astype(F32).reshape(2, 2, half)
    w1 = jnp.stack([cmp_w1_k, cmp_w1_v]).astype(BF16)
    w2 = jnp.stack([cmp_w2_k, cmp_w2_v]).astype(BF16)
    kvc = _compress(chunks, pos, w1, w2)
    o_c, q_aug, k_aug = _cmp_attention(p64, kvc, overlap)
    o_s = _sel_attention(q_aug, k_aug, p64, bias_c[0])
    o_w = _win_attention(p64, bias_c[1])
    return o_a, o_b, o_c, o_s, o_w, gate


def _layer(x, l, final, ln1, w_in, diff_lambda, diff_subln, fox_bf, cmp_pos_k, cmp_w1_k, cmp_w2_k,
           cmp_pos_v, cmp_w1_v, cmp_w2_v, bias_a, bias_c, overlap, w_out, ln2, wg, bg, we, be,
           w_gate, w_up, w_down, ln_f):
    S = x.shape[0]
    o_a, o_b, o_c, o_s, o_w, gate = _attention_block(
        x, l, ln1, w_in, diff_lambda, diff_subln, fox_bf, cmp_pos_k, cmp_w1_k, cmp_w2_k,
        cmp_pos_v, cmp_w1_v, cmp_w2_v, bias_a, bias_c, overlap)
    wr = jnp.pad(jnp.concatenate([wg, we], axis=1).astype(F32), ((0, 0), (0, 128 - N_GROUPS - N_EXPERTS)))
    br = jnp.pad(jnp.concatenate([bg, be]).astype(F32), (0, 128 - N_GROUPS - N_EXPERTS)).reshape(1, 128)
    x1, h2, ri, rw, counts = _outproj(o_a, o_b, o_c, o_s, o_w, gate, x, w_out.astype(BF16),
                                      ln2.reshape(1, D_MODEL), wr, br)
    dest, chunk_e, n_used, n_chunks = _moe_plan(ri, counts)
    xbuf = _dispatch(dest, h2, jnp.zeros((n_chunks * MOE_CHUNK, D_MODEL), F32))
    ybuf = _experts(chunk_e + l * N_EXPERTS, n_used, xbuf, w_gate, w_up, w_down)
    return _combine(dest, x1, rw, ln_f.reshape(1, D_MODEL), ybuf, final)


def _make_biases(t5_table, S):
    t5_a, t5_c = t5_table[:, :DIFF_HEADS], t5_table[:, DIFF_HEADS:]
    c_t = min(C_T, S)
    return (_bias_tiles(t5_a, min(A_TQ, S), min(A_TK, S)),
            (_bias_tiles(t5_c, c_t, min(SEL_TK, S)), _bias_tiles(t5_c, c_t, c_t)))


def kernel(x, ln1, w_in, diff_lambda, diff_subln, fox_bf, cmp_pos_k, cmp_w1_k, cmp_w2_k, cmp_pos_v, cmp_w1_v,
           cmp_w2_v, t5_table, w_out, ln2, router_group_w, router_group_b, router_expert_w, router_expert_b,
           w_gate, w_up, w_down, ln_f):
    B, S, _ = x.shape
    assert B == 1
    depth = w_in.shape[0]
    bias_a, bias_c = _make_biases(t5_table, S)
    overlap = _overlap_matrix(S // CMP_STRIDE)
    xs = x.reshape(S, D_MODEL)
    wg_all = w_gate.reshape(depth * N_EXPERTS, D_MODEL, D_EXPERT)
    wu_all = w_up.reshape(depth * N_EXPERTS, D_MODEL, D_EXPERT)
    wd_all = w_down.reshape(depth * N_EXPERTS, D_EXPERT, D_MODEL)
    for l in range(depth):
        xs = _layer(xs, l, l == depth - 1, ln1[l], w_in[l], diff_lambda[l], diff_subln[l], fox_bf[l],
                    cmp_pos_k[l], cmp_w1_k[l], cmp_w2_k[l], cmp_pos_v[l], cmp_w1_v[l], cmp_w2_v[l],
                    bias_a, bias_c, overlap, w_out[l], ln2[l], router_group_w[l], router_group_b[l],
                    router_expert_w[l], router_expert_b[l], wg_all, wu_all, wd_all, ln_f)
    return xs.reshape(B, S, D_MODEL)
```

```python
import functools
import math

import numpy as np
import jax
import jax.numpy as jnp
from jax import lax
from jax.experimental import pallas as pl
from jax.experimental.pallas import tpu as pltpu

F32 = jnp.float32
BF16 = jnp.bfloat16
I32 = jnp.int32

D_MODEL = 2048
HEAD_DIM = 64
DIFF_HEADS = 4
FOX_HEADS = 8
NSA_HEADS = 16
NSA_GROUPS = 4
NSA_HPG = 4
CMP_BLOCK = 32
CMP_STRIDE = 16
CMP_HIDDEN = 256
SEL_BLOCK = 64
SEL_TOPK = 16
WINDOW = 512
NUM_BUCKETS = 32
MAX_DISTANCE = 128
N_GROUPS = 8
EXPERTS_PER_GROUP = 8
N_EXPERTS = 64
D_EXPERT = 256
MOE_CHUNK = 256
RMS_EPS = 1e-6
NEG_INF = -1e30
SEL_FORCE = 1e4
SCALE = HEAD_DIM ** -0.5

N_MAIN = 5632
B_FOX_F = 3072
B_NSA_G = 5640
GATE_COLS = 128
SEL_LANES = 128
SEL_MASKED = -32768.0
PICKED = -3.0e38

OFF_AQ, OFF_AK, OFF_AV = 0, 8, 16
OFF_BQ, OFF_BK, OFF_BV = 24, 32, 40
OFF_CQ = 48
OFF_CKC, OFF_CVC, OFF_CKS, OFF_CVS, OFF_CKW, OFF_CVW = 64, 68, 72, 76, 80, 84

VMEM_LIMIT = 56 * 1024 * 1024


def _cparams(n_axes):
    return pltpu.CompilerParams(dimension_semantics=("arbitrary",) * n_axes,
                                vmem_limit_bytes=VMEM_LIMIT)


IN_TM = 1024
IN_TN = 512


def _inproj_kernel(x_ref, g_ref, w_ref, wg_ref, o64_ref, og_ref, h_ref):
    j = pl.program_id(1)

    @pl.when(j == 0)
    def _():
        x = x_ref[...]
        ms = jnp.mean(x * x, axis=-1, keepdims=True)
        h_ref[...] = (x * lax.rsqrt(ms + RMS_EPS) * g_ref[...]).astype(BF16)

    res = jnp.dot(h_ref[...], w_ref[...], preferred_element_type=F32)
    for b in range(IN_TN // HEAD_DIM):
        o64_ref[b] = res[:, b * HEAD_DIM:(b + 1) * HEAD_DIM].astype(BF16)

    @pl.when(j == pl.num_programs(1) - 1)
    def _():
        og_ref[...] = jnp.dot(h_ref[...], wg_ref[...], preferred_element_type=F32)


def _inproj(x, g, w_main, w_gate):
    S = x.shape[0]
    tm = min(IN_TM, S)
    nj = N_MAIN // IN_TN
    return pl.pallas_call(
        _inproj_kernel,
        grid=(S // tm, nj),
        in_specs=[
            pl.BlockSpec((tm, D_MODEL), lambda i, j: (i, 0)),
            pl.BlockSpec((1, D_MODEL), lambda i, j: (0, 0)),
            pl.BlockSpec((D_MODEL, IN_TN), lambda i, j: (0, j)),
            pl.BlockSpec((D_MODEL, GATE_COLS), lambda i, j: (0, 0)),
        ],
        out_specs=[
            pl.BlockSpec((IN_TN // HEAD_DIM, tm, HEAD_DIM), lambda i, j: (j, i, 0)),
            pl.BlockSpec((tm, GATE_COLS), lambda i, j: (i, 0)),
        ],
        out_shape=[
            jax.ShapeDtypeStruct((N_MAIN // HEAD_DIM, S, HEAD_DIM), BF16),
            jax.ShapeDtypeStruct((S, GATE_COLS), F32),
        ],
        scratch_shapes=[pltpu.VMEM((tm, D_MODEL), BF16)],
        compiler_params=_cparams(2),
        name="inproj",
    )(x, g, w_main, w_gate)


CUM_TB = 512


def _split3(x):
    hi = x.astype(BF16)
    r1 = x - hi.astype(F32)
    mid = r1.astype(BF16)
    lo = (r1 - mid.astype(F32)).astype(BF16)
    return hi, mid, lo


def _cumsum_kernel(gate_ref, bf_ref, q_ref, k_ref, qa_ref, ka_ref, carry_ref):
    i = pl.program_id(0)

    @pl.when(i == 0)
    def _():
        carry_ref[...] = jnp.zeros_like(carry_ref)

    z = gate_ref[...] + bf_ref[...]
    log_f = -(jnp.maximum(-z, 0.0) + jnp.log1p(jnp.exp(-jnp.abs(z))))
    tb = log_f.shape[0]
    row = lax.broadcasted_iota(I32, (tb, tb), 0)
    col = lax.broadcasted_iota(I32, (tb, tb), 1)
    tri = jnp.where(row >= col, 1.0, 0.0).astype(BF16)
    hi, mid, lo = _split3(log_f)
    c = (jnp.dot(tri, hi, preferred_element_type=F32)
         + jnp.dot(tri, mid, preferred_element_type=F32)
         + jnp.dot(tri, lo, preferred_element_type=F32)) + carry_ref[...]
    carry_ref[...] = c[tb - 1:tb, :]
    c2 = c * LOG2E
    lane = lax.broadcasted_iota(I32, (tb, HEAD_DIM), 1)
    for h in range(FOX_HEADS):
        parts = [t.astype(F32) for t in _split3(c2[:, h:h + 1])]
        tail_q = jnp.where((lane >= 3) & (lane < 6), 1.0, 0.0)
        tail_k = jnp.where(lane < 3, 1.0, 0.0)
        for j, t in enumerate(parts):
            tail_q = jnp.where(lane == j, t, tail_q)
            tail_k = jnp.where(lane == 3 + j, -t, tail_k)
        qa_ref[h, :, 0:HEAD_DIM] = q_ref[h] * (SCALE * LOG2E)
        qa_ref[h, :, HEAD_DIM:] = tail_q.astype(BF16)
        ka_ref[h, :, 0:HEAD_DIM] = k_ref[h]
        ka_ref[h, :, HEAD_DIM:] = tail_k.astype(BF16)


def _fox_cumsum(gate, bf_row, p64):
    S = gate.shape[0]
    tb = min(CUM_TB, S)
    assert OFF_BQ % FOX_HEADS == 0 and OFF_BK % FOX_HEADS == 0
    out = jax.ShapeDtypeStruct((FOX_HEADS, S, 2 * HEAD_DIM), BF16)
    return pl.pallas_call(
        _cumsum_kernel,
        grid=(S // tb,),
        in_specs=[pl.BlockSpec((tb, GATE_COLS), lambda i: (i, 0)),
                  pl.BlockSpec((1, GATE_COLS), lambda i: (0, 0)),
                  pl.BlockSpec((FOX_HEADS, tb, HEAD_DIM), lambda i: (OFF_BQ // FOX_HEADS, i, 0)),
                  pl.BlockSpec((FOX_HEADS, tb, HEAD_DIM), lambda i: (OFF_BK // FOX_HEADS, i, 0))],
        out_specs=[pl.BlockSpec((FOX_HEADS, tb, 2 * HEAD_DIM), lambda i: (0, i, 0)),
                   pl.BlockSpec((FOX_HEADS, tb, 2 * HEAD_DIM), lambda i: (0, i, 0))],
        out_shape=[out, out],
        scratch_shapes=[pltpu.VMEM((1, GATE_COLS), F32)],
        compiler_params=_cparams(1),
        name="fox_cumsum",
    )(gate, bf_row, p64, p64)


def _tile_pairs(S, tq, tk):
    assert tk % tq == 0
    qi_l, ki_l, cls_l, last_l = [], [], [], []
    for qi in range(S // tq):
        n_k = (qi * tq + tq - 1) // tk + 1
        for ki in range(n_k):
            qi_l.append(qi)
            ki_l.append(ki)
            cls_l.append((qi * tq - ki * tk) // tq)
            last_l.append(int(ki == n_k - 1))
    return tuple(jnp.asarray(np.asarray(a, np.int32)) for a in (qi_l, ki_l, cls_l, last_l))


def _n_bias_classes(tq, tk):
    return -(-(tk + MAX_DISTANCE) // tq)


STAT_LANES = 128
LOG2E = 1.4426950408889634


def _chunk_rows(tk):
    return max(16, min(32, 16384 // tk))


def _flash_chains(chains, s_ref, p_ref, alpha_ref, m_ref, l_ref, acc_ref):
    width = acc_ref.shape[1]

    def park(chain):
        scores_fn, _, row0, nrows, _ = chain
        s_ref[row0:row0 + nrows, :] = scores_fn()

    def softmax_pv(chain):
        _, v, row0, nrows, addend = chain
        tk = s_ref.shape[1]
        reps = tk // STAT_LANES
        rc = _chunk_rows(tk)

        def chunk(c):
            sl = slice(row0 + c * rc, row0 + (c + 1) * rc)
            s = s_ref[sl, :]
            return sl, (s if addend is None else s + addend(c * rc, rc))

        for c in range(nrows // rc):
            sl, s = chunk(c)
            m_prev = m_ref[sl, :]
            m_new = jnp.maximum(m_prev, jnp.max(s, axis=1, keepdims=True))
            alpha_ref[sl, :] = jnp.exp2(m_prev - m_new)
            m_ref[sl, :] = m_new
        for c in range(nrows // rc):
            sl, s = chunk(c)
            p = jnp.exp2(s - jnp.tile(m_ref[sl, :], (1, reps)))
            part = p[:, 0:STAT_LANES]
            for r in range(1, reps):
                part = part + p[:, r * STAT_LANES:(r + 1) * STAT_LANES]
            l_ref[sl, :] = alpha_ref[sl, :] * l_ref[sl, :] + part
            p_ref[sl, :] = p.astype(BF16)
        rows = slice(row0, row0 + nrows)
        acc_ref[rows, :] = (alpha_ref[rows, 0:width] * acc_ref[rows, :]
                            + jnp.dot(p_ref[rows, :], v, preferred_element_type=F32))

    park(chains[0])
    for i, chain in enumerate(chains):
        if i + 1 < len(chains):
            park(chains[i + 1])
        softmax_pv(chain)


def _row_sum(l):
    return jnp.broadcast_to(jnp.sum(l, axis=1, keepdims=True), l.shape)


def _init_state(m_ref, l_ref, acc_ref):
    m_ref[...] = jnp.full_like(m_ref, NEG_INF)
    l_ref[...] = jnp.zeros_like(l_ref)
    acc_ref[...] = jnp.zeros_like(acc_ref)


def _qk(q, k):
    return lax.dot_general(q, k, (((1,), (1,)), ((), ())), preferred_element_type=F32)


def _t5_bucket(rel):
    n = jnp.maximum(rel, 0)
    max_exact = NUM_BUCKETS // 2
    nf = jnp.maximum(n, max_exact).astype(F32)
    large = max_exact + (jnp.log(nf / max_exact) / math.log(MAX_DISTANCE / max_exact)
                         * (NUM_BUCKETS - max_exact)).astype(I32)
    return jnp.where(n < max_exact, n, jnp.minimum(large, NUM_BUCKETS - 1))


def _bias_tiles(table, tq, tk):
    n_cls = _n_bias_classes(tq, tk)
    i = jnp.arange(tq)[:, None]
    j = jnp.arange(tk)[None, :]
    rel = jnp.stack([c * tq + i - j for c in range(n_cls)])
    t = table.astype(F32)
    bucket = _t5_bucket(rel)[..., None]
    b = jnp.zeros(rel.shape + (t.shape[1],), F32)
    for k in range(NUM_BUCKETS - 1):
        b = jnp.where(bucket == k, (t[k] - t[NUM_BUCKETS - 1]) * LOG2E, b)
    b = jnp.where((rel >= 0)[..., None], b, NEG_INF)
    return jnp.transpose(b, (3, 0, 1, 2))


A_TQ = 1024
A_TK = 1024


def _diff_kernel(n_cls, qt_ref, kt_ref, cls_ref, last_ref, q1_ref, q2_ref, k1_ref, k2_ref, v_ref, bias_ref,
                 dl_ref, sub_ref, cst_ref, o_ref, qs, s_ref, p_ref, alpha, m, l, acc):
    p = pl.program_id(1)
    ki = kt_ref[p]
    T = q1_ref.shape[1]
    half = T // 2

    @pl.when(ki == 0)
    def _():
        _init_state(m, l, acc)
        qs[0:T, :] = q1_ref[0] * (SCALE * LOG2E)
        qs[T:2 * T, :] = q2_ref[0] * (SCALE * LOG2E)

    def step(with_bias):
        chains = []
        for j, k_ref in enumerate((k1_ref, k2_ref)):
            for hh in range(2):
                r0 = j * T + hh * half
                bias = ((lambda r, n, hh=hh: bias_ref[0, 0, hh * half + r:hh * half + r + n, :])
                        if with_bias else None)
                chains.append((lambda r0=r0, k_ref=k_ref: _qk(qs[r0:r0 + half, :], k_ref[0]),
                               v_ref[0], r0, half, bias))
        _flash_chains(chains, s_ref, p_ref, alpha, m, l, acc)

    @pl.when(cls_ref[p] < n_cls)
    def _():
        step(True)

    @pl.when(cls_ref[p] >= n_cls)
    def _():
        step(False)

    @pl.when(last_ref[p] == 1)
    def _():
        dl = dl_ref[...]
        lam_init = cst_ref[0:1, 0:1]
        lam = (jnp.exp(jnp.sum(dl[0:1] * dl[1:2], axis=1, keepdims=True))
               - jnp.exp(jnp.sum(dl[2:3] * dl[3:4], axis=1, keepdims=True)) + lam_init)
        o = acc[0:T, :] / _row_sum(l[0:T, :]) - lam * (acc[T:2 * T, :] / _row_sum(l[T:2 * T, :]))
        ms = jnp.mean(o * o, axis=-1, keepdims=True)
        y = o * lax.rsqrt(ms + RMS_EPS) * sub_ref[...]
        o_ref[0] = (y * (1.0 - lam_init)).astype(BF16)


def _diff_attention(p64, v128, bias, dl, subln, cst):
    S = p64.shape[1]
    T, TK = bias.shape[2], bias.shape[3]
    n_cls = bias.shape[1]
    tabs = _tile_pairs(S, T, TK)
    qblk = lambda off: pl.BlockSpec((1, T, HEAD_DIM), lambda h, p, qt, kt, cl, la: (off + 2 * h, qt[p], 0))
    kblk = lambda off: pl.BlockSpec((1, TK, HEAD_DIM), lambda h, p, qt, kt, cl, la: (off + 2 * h, kt[p], 0))
    const = lambda shape: pl.BlockSpec(shape, lambda h, p, qt, kt, cl, la: (0,) * len(shape))
    grid_spec = pltpu.PrefetchScalarGridSpec(
        num_scalar_prefetch=4,
        grid=(DIFF_HEADS, int(tabs[0].shape[0])),
        in_specs=[
            qblk(OFF_AQ), qblk(OFF_AQ + 1), kblk(OFF_AK), kblk(OFF_AK + 1),
            pl.BlockSpec((1, TK, 2 * HEAD_DIM), lambda h, p, qt, kt, cl, la: (h, kt[p], 0)),
            pl.BlockSpec((1, 1, T, TK), lambda h, p, qt, kt, cl, la: (h, jnp.minimum(cl[p], n_cls - 1), 0, 0)),
            const((4, HEAD_DIM)), const((1, 2 * HEAD_DIM)), const((1, 128)),
        ],
        out_specs=pl.BlockSpec((1, T, 2 * HEAD_DIM), lambda h, p, qt, kt, cl, la: (h, qt[p], 0)),
        scratch_shapes=[
            pltpu.VMEM((2 * T, HEAD_DIM), BF16),
            pltpu.VMEM((2 * T, TK), F32), pltpu.VMEM((2 * T, TK), BF16), pltpu.VMEM((2 * T, STAT_LANES), F32),
            pltpu.VMEM((2 * T, STAT_LANES), F32), pltpu.VMEM((2 * T, STAT_LANES), F32),
            pltpu.VMEM((2 * T, 2 * HEAD_DIM), F32),
        ],
    )
    return pl.pallas_call(
        functools.partial(_diff_kernel, n_cls),
        grid_spec=grid_spec,
        out_shape=jax.ShapeDtypeStruct((DIFF_HEADS, S, 2 * HEAD_DIM), BF16),
        compiler_params=_cparams(2),
        name="diff_attn",
    )(*tabs, p64, p64, p64, p64, v128, bias, dl, subln, cst)


B_TQ = 1024
B_TK = 1024


def _fox_kernel(diag_cls, qt_ref, kt_ref, cls_ref, last_ref, q_ref, k_ref, v_ref, o_ref,
                s_ref, p_ref, alpha, m, l, acc):
    p = pl.program_id(1)
    T = q_ref.shape[1]
    TK = k_ref.shape[1]
    half = T // 2

    @pl.when(kt_ref[p] == 0)
    def _():
        _init_state(m, l, acc)

    def step(diag):
        lead = cls_ref[p] * T
        chains = []
        for hh in range(2):
            r0 = hh * half

            def causal(r, n, r0=r0):
                row = lead + (r0 + r) + lax.broadcasted_iota(I32, (n, TK), 0)
                col = lax.broadcasted_iota(I32, (n, TK), 1)
                return jnp.where(row >= col, 0.0, NEG_INF)

            chains.append((lambda r0=r0: _qk(q_ref[0, r0:r0 + half, :], k_ref[0]), v_ref[0], r0, half,
                           causal if diag else None))
        _flash_chains(chains, s_ref, p_ref, alpha, m, l, acc)

    @pl.when(cls_ref[p] < diag_cls)
    def _():
        step(True)

    @pl.when(cls_ref[p] >= diag_cls)
    def _():
        step(False)

    @pl.when(last_ref[p] == 1)
    def _():
        o_ref[0] = (acc[...] / _row_sum(l[...])[:, 0:HEAD_DIM]).astype(BF16)


def _fox_attention(qa, ka, p64):
    S = p64.shape[1]
    T = min(B_TQ, S)
    TK = min(B_TK, S)
    aug = qa.shape[2]
    tabs = _tile_pairs(S, T, TK)
    grid_spec = pltpu.PrefetchScalarGridSpec(
        num_scalar_prefetch=4,
        grid=(FOX_HEADS, int(tabs[0].shape[0])),
        in_specs=[
            pl.BlockSpec((1, T, aug), lambda h, p, qt, kt, cl, la: (h, qt[p], 0)),
            pl.BlockSpec((1, TK, aug), lambda h, p, qt, kt, cl, la: (h, kt[p], 0)),
            pl.BlockSpec((1, TK, HEAD_DIM), lambda h, p, qt, kt, cl, la: (OFF_BV + h, kt[p], 0)),
        ],
        out_specs=pl.BlockSpec((1, T, HEAD_DIM), lambda h, p, qt, kt, cl, la: (h, qt[p], 0)),
        scratch_shapes=[
            pltpu.VMEM((T, TK), F32), pltpu.VMEM((T, TK), BF16), pltpu.VMEM((T, STAT_LANES), F32),
            pltpu.VMEM((T, STAT_LANES), F32), pltpu.VMEM((T, STAT_LANES), F32), pltpu.VMEM((T, HEAD_DIM), F32),
        ],
    )
    return pl.pallas_call(
        functools.partial(_fox_kernel, TK // T),
        grid_spec=grid_spec,
        out_shape=jax.ShapeDtypeStruct((FOX_HEADS, S, HEAD_DIM), BF16),
        compiler_params=_cparams(2),
        name="fox_attn",
    )(*tabs, qa, ka, p64)


def _gelu_tanh(x):
    return 0.5 * x * (1.0 + jnp.tanh(math.sqrt(2.0 / math.pi) * (x + 0.044715 * (x * x * x))))


def _compress_kernel(x_ref, pos_ref, w1_ref, w2_ref, o_ref):
    x = x_ref[0].astype(F32)
    half = x.shape[1]
    pos = pos_ref[0]
    x_lo = (x + pos[0:1]).astype(BF16)
    x_hi = (x + pos[1:2]).astype(BF16)
    w1 = w1_ref[0]
    y_lo = jnp.dot(x_lo, w1[:half], preferred_element_type=F32)
    y_hi = jnp.dot(x_hi, w1[half:], preferred_element_type=F32)
    n = y_hi.shape[0]
    pre = y_lo + pltpu.roll(y_hi, n - 1, 0)
    o_ref[0] = jnp.dot(_gelu_tanh(pre).astype(BF16), w2_ref[0], preferred_element_type=F32).astype(BF16)


def _compress(chunks, pos, w1, w2):
    n = chunks.shape[1]
    half = CMP_STRIDE * HEAD_DIM
    return pl.pallas_call(
        _compress_kernel,
        grid=(2 * NSA_GROUPS,),
        in_specs=[
            pl.BlockSpec((1, n, half), lambda i: (i, 0, 0)),
            pl.BlockSpec((1, 2, half), lambda i: (i // NSA_GROUPS, 0, 0)),
            pl.BlockSpec((1, 2 * half, CMP_HIDDEN), lambda i: (i // NSA_GROUPS, 0, 0)),
            pl.BlockSpec((1, CMP_HIDDEN, HEAD_DIM), lambda i: (i // NSA_GROUPS, 0, 0)),
        ],
        out_specs=pl.BlockSpec((1, n, HEAD_DIM), lambda i: (i, 0, 0)),
        out_shape=jax.ShapeDtypeStruct((2 * NSA_GROUPS, n, HEAD_DIM), BF16),
        compiler_params=_cparams(1),
        name="nsa_compress",
    )(chunks, pos, w1, w2)


C_T = 256


def _cmp_kernel(q_ref, kc_ref, vc_ref, ks_ref, ov_ref, oc_ref, qa_ref, ka_ref):
    qi = pl.program_id(1)
    T = q_ref.shape[1]
    ncp = kc_ref.shape[1]
    q0 = qi * T
    qs = (q_ref[...] * SCALE).reshape(NSA_HPG * T, HEAD_DIM)
    s = _qk(qs, kc_ref[0]).reshape(NSA_HPG, T, ncp)
    t = q0 + lax.broadcasted_iota(I32, (T, ncp), 0)
    n_idx = lax.broadcasted_iota(I32, (T, ncp), 1)
    visible = (CMP_STRIDE * n_idx + (CMP_BLOCK - 1) <= t)[None]
    s = jnp.where(visible, s, NEG_INF)
    smax = jnp.max(s, axis=2, keepdims=True)
    e = jnp.where(visible, jnp.exp(s - smax), 0.0)
    den = jnp.sum(e, axis=2, keepdims=True)
    p = e / jnp.where(den > 0.0, den, 1.0)
    o = jnp.dot(p.reshape(NSA_HPG * T, ncp).astype(BF16), vc_ref[0], preferred_element_type=F32)
    oc_ref[...] = o.reshape(NSA_HPG, T, HEAD_DIM).astype(BF16)

    psum = p[0] + p[1] + p[2] + p[3]
    hi, mid, lo = _split3(psum)
    ov = ov_ref[...]
    imp = (jnp.dot(hi, ov, preferred_element_type=F32) + jnp.dot(mid, ov, preferred_element_type=F32)
           + jnp.dot(lo, ov, preferred_element_type=F32))
    tq = q0 + lax.broadcasted_iota(I32, (T, SEL_LANES), 0)
    jb = lax.broadcasted_iota(I32, (T, SEL_LANES), 1)
    cur = tq // SEL_BLOCK
    forced = (jb == 0) | (jb == cur) | (jb == cur - 1)
    valid = SEL_BLOCK * jb <= tq
    score = jnp.where(valid, imp + jnp.where(forced, SEL_FORCE, 0.0), NEG_INF)
    sc = score.T
    jrow = lax.broadcasted_iota(I32, sc.shape, 0).astype(F32)
    sel = jnp.zeros(sc.shape, F32)
    for _ in range(SEL_TOPK):
        mx = jnp.max(sc, axis=0, keepdims=True)
        first = jnp.min(jnp.where(sc == mx, jrow, float(SEL_LANES)), axis=0, keepdims=True)
        pick = jrow == first
        sel = jnp.where(pick, 1.0, sel)
        sc = jnp.where(pick, PICKED, sc)
    sel_neg = jnp.where(sel.T > 0.5, 0.0, SEL_MASKED).astype(BF16)

    zeros = jnp.zeros((T, HEAD_DIM), BF16)
    for hh in range(NSA_HPG):
        qa_ref[hh, :, 0:HEAD_DIM] = q_ref[hh] * (SCALE * LOG2E)
        qa_ref[hh, :, HEAD_DIM:2 * HEAD_DIM] = zeros
        qa_ref[hh, :, 2 * HEAD_DIM:] = sel_neg
    kblk = (q0 + lax.broadcasted_iota(I32, (T, SEL_LANES), 0)) // SEL_BLOCK
    onehot = jnp.where(kblk == jb, 1.0, 0.0).astype(BF16)
    ka_ref[0, :, 0:HEAD_DIM] = ks_ref[0]
    ka_ref[0, :, HEAD_DIM:2 * HEAD_DIM] = zeros
    ka_ref[0, :, 2 * HEAD_DIM:] = onehot


def _cmp_attention(p64, kvc, overlap):
    S = p64.shape[1]
    T = min(C_T, S)
    ncp = kvc.shape[1]
    aug = 2 * HEAD_DIM + SEL_LANES
    return pl.pallas_call(
        _cmp_kernel,
        grid=(NSA_GROUPS, S // T),
        in_specs=[
            pl.BlockSpec((NSA_HPG, T, HEAD_DIM), lambda g, i: (OFF_CQ // NSA_HPG + g, i, 0)),
            pl.BlockSpec((1, ncp, HEAD_DIM), lambda g, i: (g, 0, 0)),
            pl.BlockSpec((1, ncp, HEAD_DIM), lambda g, i: (NSA_GROUPS + g, 0, 0)),
            pl.BlockSpec((1, T, HEAD_DIM), lambda g, i: (OFF_CKS + g, i, 0)),
            pl.BlockSpec((ncp, SEL_LANES), lambda g, i: (0, 0)),
        ],
        out_specs=[
            pl.BlockSpec((NSA_HPG, T, HEAD_DIM), lambda g, i: (g, i, 0)),
            pl.BlockSpec((NSA_HPG, T, aug), lambda g, i: (g, i, 0)),
            pl.BlockSpec((1, T, aug), lambda g, i: (g, i, 0)),
        ],
        out_shape=[
            jax.ShapeDtypeStruct((NSA_HEADS, S, HEAD_DIM), BF16),
            jax.ShapeDtypeStruct((NSA_HEADS, S, aug), BF16),
            jax.ShapeDtypeStruct((NSA_GROUPS, S, aug), BF16),
        ],
        compiler_params=_cparams(2),
        name="nsa_cmp_attn",
    )(p64, kvc, kvc, p64, overlap)


def _overlap_matrix(ncp):
    n = np.arange(ncp)[:, None] * CMP_STRIDE
    j = np.arange(SEL_LANES)[None, :] * SEL_BLOCK
    ov = np.clip(np.minimum(n + CMP_BLOCK, j + SEL_BLOCK) - np.maximum(n, j), 0, None) / CMP_BLOCK
    ov[ncp - 1:] = 0.0
    return jnp.asarray(ov, BF16)


SEL_TQ = 512
SEL_TK = 512


def _sel_kernel(n_cls, qt_ref, kt_ref, cls_ref, last_ref, qa_ref, ka_ref, v_ref, bias_ref, o_ref,
                s_ref, p_ref, alpha, m, l, acc):
    p = pl.program_id(1)
    T = qa_ref.shape[1]

    @pl.when(kt_ref[p] == 0)
    def _():
        _init_state(m, l, acc)

    def step(with_bias):
        chains = []
        for h in range(NSA_HPG):
            bias = (lambda r, n, h=h: bias_ref[h, 0, r:r + n, :]) if with_bias else None
            chains.append((lambda h=h: _qk(qa_ref[h], ka_ref[0]), v_ref[0], h * T, T, bias))
        _flash_chains(chains, s_ref, p_ref, alpha, m, l, acc)

    @pl.when(cls_ref[p] < n_cls)
    def _():
        step(True)

    @pl.when(cls_ref[p] >= n_cls)
    def _():
        step(False)

    @pl.when(last_ref[p] == 1)
    def _():
        o_ref[...] = (acc[...] / _row_sum(l[...])[:, :HEAD_DIM]).reshape(NSA_HPG, T, HEAD_DIM).astype(BF16)


def _sel_attention(q_aug, k_aug, p64, bias):
    S = p64.shape[1]
    n_cls, T, TK = bias.shape[1:]
    aug = q_aug.shape[2]
    tabs = _tile_pairs(S, T, TK)
    grid_spec = pltpu.PrefetchScalarGridSpec(
        num_scalar_prefetch=4,
        grid=(NSA_GROUPS, int(tabs[0].shape[0])),
        in_specs=[
            pl.BlockSpec((NSA_HPG, T, aug), lambda g, p, qt, kt, cl, la: (g, qt[p], 0)),
            pl.BlockSpec((1, TK, aug), lambda g, p, qt, kt, cl, la: (g, kt[p], 0)),
            pl.BlockSpec((1, TK, HEAD_DIM), lambda g, p, qt, kt, cl, la: (OFF_CVS + g, kt[p], 0)),
            pl.BlockSpec((NSA_HPG, 1, T, TK), lambda g, p, qt, kt, cl, la: (g, jnp.minimum(cl[p], n_cls - 1), 0, 0)),
        ],
        out_specs=pl.BlockSpec((NSA_HPG, T, HEAD_DIM), lambda g, p, qt, kt, cl, la: (g, qt[p], 0)),
        scratch_shapes=[pltpu.VMEM((NSA_HPG * T, TK), F32), pltpu.VMEM((NSA_HPG * T, TK), BF16),
                        pltpu.VMEM((NSA_HPG * T, STAT_LANES), F32),
                        pltpu.VMEM((NSA_HPG * T, STAT_LANES), F32), pltpu.VMEM((NSA_HPG * T, STAT_LANES), F32),
                        pltpu.VMEM((NSA_HPG * T, HEAD_DIM), F32)],
    )
    return pl.pallas_call(
        functools.partial(_sel_kernel, n_cls),
        grid_spec=grid_spec,
        out_shape=jax.ShapeDtypeStruct((NSA_HEADS, S, HEAD_DIM), BF16),
        compiler_params=_cparams(2),
        name="nsa_sel_attn",
    )(*tabs, q_aug, k_aug, p64, bias)


def _win_kernel(q_ref, k_ref, v_ref, bias_ref, o_ref, qs, s_ref, p_ref, alpha, m, l, acc):
    qi = pl.program_id(1)
    kk = pl.program_id(2)
    T = q_ref.shape[1]

    @pl.when(kk == 0)
    def _():
        _init_state(m, l, acc)
        qs[...] = (q_ref[...] * (SCALE * LOG2E)).reshape(NSA_HPG * T, HEAD_DIM)

    def step(kind):
        chains = []
        for h in range(NSA_HPG):
            if kind < 2:
                bias = lambda r, n, h=h: bias_ref[h, 0, r:r + n, :]
            else:
                def bias(r, n):
                    row = r + lax.broadcasted_iota(I32, (n, T), 0)
                    col = lax.broadcasted_iota(I32, (n, T), 1)
                    return jnp.where(col > row, 0.0, NEG_INF)
            chains.append((lambda h=h: _qk(qs[h * T:(h + 1) * T, :], k_ref[0]), v_ref[0], h * T, T, bias))
        _flash_chains(chains, s_ref, p_ref, alpha, m, l, acc)

    @pl.when(kk == 0)
    def _():
        step(0)

    @pl.when((kk == 1) & (qi >= 1))
    def _():
        step(1)

    @pl.when((kk == 2) & (qi >= 2))
    def _():
        step(2)

    @pl.when(kk == 2)
    def _():
        o_ref[...] = (acc[...] / _row_sum(l[...])[:, 0:HEAD_DIM]).reshape(NSA_HPG, T, HEAD_DIM).astype(BF16)


def _win_attention(p64, bias):
    S = p64.shape[1]
    T = min(C_T, S)
    assert WINDOW == 2 * T
    return pl.pallas_call(
        _win_kernel,
        grid=(NSA_GROUPS, S // T, 3),
        in_specs=[
            pl.BlockSpec((NSA_HPG, T, HEAD_DIM), lambda g, i, kk: (OFF_CQ // NSA_HPG + g, i, 0)),
            pl.BlockSpec((1, T, HEAD_DIM), lambda g, i, kk: (OFF_CKW + g, jnp.maximum(i - kk, 0), 0)),
            pl.BlockSpec((1, T, HEAD_DIM), lambda g, i, kk: (OFF_CVW + g, jnp.maximum(i - kk, 0), 0)),
            pl.BlockSpec((NSA_HPG, 1, T, T), lambda g, i, kk: (g, jnp.minimum(kk, 1), 0, 0)),
        ],
        out_specs=pl.BlockSpec((NSA_HPG, T, HEAD_DIM), lambda g, i, kk: (g, i, 0)),
        out_shape=jax.ShapeDtypeStruct((NSA_HEADS, S, HEAD_DIM), BF16),
        scratch_shapes=[pltpu.VMEM((NSA_HPG * T, HEAD_DIM), BF16),
                        pltpu.VMEM((NSA_HPG * T, T), F32), pltpu.VMEM((NSA_HPG * T, T), BF16),
                        pltpu.VMEM((NSA_HPG * T, STAT_LANES), F32),
                        pltpu.VMEM((NSA_HPG * T, STAT_LANES), F32), pltpu.VMEM((NSA_HPG * T, STAT_LANES), F32),
                        pltpu.VMEM((NSA_HPG * T, HEAD_DIM), F32)],
        compiler_params=_cparams(3),
        name="nsa_win_attn",
    )(p64, p64, p64, bias)


OUT_TM = 256
RT_E1, RT_E2, RT_R1, RT_R2 = 0, 1, 2, 3
ROUTER_LANE0 = N_GROUPS


def _outproj_kernel(oa_ref, ob_ref, oc_ref, os_ref, ow_ref, gate_ref, x_ref, w_ref, ln_ref, wr_ref, br_ref,
                    x1_ref, h2_ref, ri_ref, rw_ref, cnt_ref, mix_ref, carry_ref):
    i = pl.program_id(0)
    tm = x_ref.shape[0]

    @pl.when(i == 0)
    def _():
        carry_ref[...] = jnp.zeros_like(carry_ref)

    for h in range(DIFF_HEADS):
        mix_ref[:, 128 * h:128 * (h + 1)] = oa_ref[h]
    for h in range(FOX_HEADS):
        c0 = 512 + HEAD_DIM * h
        mix_ref[:, c0:c0 + HEAD_DIM] = ob_ref[h]
    sig = jax.nn.sigmoid(gate_ref[...])
    for h in range(NSA_HEADS):
        c0 = 1024 + HEAD_DIM * h
        g0 = sig[:, FOX_HEADS + h:FOX_HEADS + h + 1]
        g1 = sig[:, FOX_HEADS + NSA_HEADS + h:FOX_HEADS + NSA_HEADS + h + 1]
        g2 = sig[:, FOX_HEADS + 2 * NSA_HEADS + h:FOX_HEADS + 2 * NSA_HEADS + h + 1]
        o = g0 * oc_ref[h].astype(F32) + g1 * os_ref[h].astype(F32) + g2 * ow_ref[h].astype(F32)
        mix_ref[:, c0:c0 + HEAD_DIM] = o.astype(BF16)

    x1 = x_ref[...] + jnp.dot(mix_ref[...], w_ref[...], preferred_element_type=F32)
    x1_ref[...] = x1
    ms = jnp.mean(x1 * x1, axis=-1, keepdims=True)
    h2 = x1 * lax.rsqrt(ms + RMS_EPS) * ln_ref[...]
    h2_ref[...] = h2

    a0, a1, a2 = _split3(h2)
    b0, b1, b2 = _split3(wr_ref[...])
    dot = lambda a, b: jnp.dot(a, b, preferred_element_type=F32)
    logits = (dot(a0, b0) + (dot(a0, b1) + dot(a1, b0))
              + (dot(a0, b2) + dot(a1, b1) + dot(a2, b0))) + br_ref[...]
    lane = lax.broadcasted_iota(I32, logits.shape, 1)
    lane_f = lane.astype(F32)

    def first_lane(cond):
        return jnp.min(jnp.where(cond, lane_f, 128.0), axis=1, keepdims=True).astype(I32)

    is_g = lane < N_GROUPS
    lg = jnp.where(is_g, logits, NEG_INF)
    gmax = jnp.max(lg, axis=1, keepdims=True)
    gsum = jnp.sum(jnp.where(is_g, jnp.exp(lg - gmax), 0.0), axis=1, keepdims=True)
    g_p = 1.0 / gsum
    g_idx = first_lane(lg == gmax)
    e_lane = lane - ROUTER_LANE0
    in_group = (e_lane >= 0) & (e_lane < N_EXPERTS) & ((e_lane >> 3) == g_idx)
    em = jnp.where(in_group, logits, NEG_INF)
    v1 = jnp.max(em, axis=1, keepdims=True)
    i1 = first_lane(em == v1)
    em2 = jnp.where(lane == i1, PICKED, em)
    v2 = jnp.max(em2, axis=1, keepdims=True)
    i2 = first_lane(em2 == v2)
    tt = jnp.exp(v2 - v1)
    w1 = g_p / (1.0 + tt)
    w2 = g_p * tt / (1.0 + tt)

    oh = jnp.where((lane == i1) | (lane == i2), 1.0, 0.0)
    row = lax.broadcasted_iota(I32, (tm, tm), 0)
    col = lax.broadcasted_iota(I32, (tm, tm), 1)
    tri = jnp.where(row > col, 1.0, 0.0).astype(BF16)
    prefix = jnp.dot(tri, oh.astype(BF16), preferred_element_type=F32) + carry_ref[...]
    r1 = jnp.sum(jnp.where(lane == i1, prefix, 0.0), axis=1, keepdims=True)
    r2 = jnp.sum(jnp.where(lane == i2, prefix, 0.0), axis=1, keepdims=True)
    carry = carry_ref[...] + jnp.sum(oh, axis=0, keepdims=True)
    carry_ref[...] = carry
    cnt_ref[...] = carry

    ri = jnp.where(lane == RT_E1, i1 - ROUTER_LANE0,
                   jnp.where(lane == RT_E2, i2 - ROUTER_LANE0,
                             jnp.where(lane == RT_R1, r1.astype(I32),
                                       jnp.where(lane == RT_R2, r2.astype(I32), 0))))
    ri_ref[...] = ri
    rw_ref[...] = jnp.where(lane == 0, w1, jnp.where(lane == 1, w2, 0.0))


def _outproj(o_a, o_b, o_c, o_s, o_w, gate, x, w_out, ln2, wr, br):
    S = x.shape[0]
    tm = min(OUT_TM, S)
    full = lambda shape: pl.BlockSpec(shape, lambda i: (0,) * len(shape))
    return pl.pallas_call(
        _outproj_kernel,
        grid=(S // tm,),
        in_specs=[
            pl.BlockSpec((DIFF_HEADS, tm, 2 * HEAD_DIM), lambda i: (0, i, 0)),
            pl.BlockSpec((FOX_HEADS, tm, HEAD_DIM), lambda i: (0, i, 0)),
            pl.BlockSpec((NSA_HEADS, tm, HEAD_DIM), lambda i: (0, i, 0)),
            pl.BlockSpec((NSA_HEADS, tm, HEAD_DIM), lambda i: (0, i, 0)),
            pl.BlockSpec((NSA_HEADS, tm, HEAD_DIM), lambda i: (0, i, 0)),
            pl.BlockSpec((tm, GATE_COLS), lambda i: (i, 0)),
            pl.BlockSpec((tm, D_MODEL), lambda i: (i, 0)),
            full((D_MODEL, D_MODEL)),
            full((1, D_MODEL)),
            full((D_MODEL, 128)),
            full((1, 128)),
        ],
        out_specs=[
            pl.BlockSpec((tm, D_MODEL), lambda i: (i, 0)),
            pl.BlockSpec((tm, D_MODEL), lambda i: (i, 0)),
            pl.BlockSpec((tm, 128), lambda i: (i, 0)),
            pl.BlockSpec((tm, 128), lambda i: (i, 0)),
            pl.BlockSpec((1, 128), lambda i: (0, 0)),
        ],
        out_shape=[
            jax.ShapeDtypeStruct((S, D_MODEL), F32),
            jax.ShapeDtypeStruct((S, D_MODEL), F32),
            jax.ShapeDtypeStruct((S, 128), I32),
            jax.ShapeDtypeStruct((S, 128), F32),
            jax.ShapeDtypeStruct((1, 128), F32),
        ],
        scratch_shapes=[pltpu.VMEM((tm, D_MODEL), BF16), pltpu.VMEM((1, 128), F32)],
        compiler_params=_cparams(1),
        name="outproj_router",
    )(o_a, o_b, o_c, o_s, o_w, gate, x, w_out, ln2, wr, br)


DISP_TM = 256


def _dispatch_kernel(dest_ref, h_ref, xb_in_ref, xb_ref, sem):
    del xb_in_ref
    i = pl.program_id(0)
    base = i * DISP_TM

    def copy(t, k):
        return pltpu.make_async_copy(h_ref.at[pl.ds(t, 1)],
                                     xb_ref.at[pl.ds(dest_ref[2 * (base + t) + k], 1)], sem)

    def start(t, c):
        copy(t, 0).start()
        copy(t, 1).start()
        return c

    def wait(t, c):
        copy(t, 0).wait()
        copy(t, 1).wait()
        return c

    lax.fori_loop(0, DISP_TM, start, 0)
    lax.fori_loop(0, DISP_TM, wait, 0)


def _dispatch(dest_flat, h2, xbuf0):
    S = h2.shape[0]
    assert S % DISP_TM == 0
    grid_spec = pltpu.PrefetchScalarGridSpec(
        num_scalar_prefetch=1,
        grid=(S // DISP_TM,),
        in_specs=[pl.BlockSpec((DISP_TM, D_MODEL), lambda i, d: (i, 0)), pl.BlockSpec(memory_space=pl.ANY)],
        out_specs=pl.BlockSpec(memory_space=pl.ANY),
        scratch_shapes=[pltpu.SemaphoreType.DMA(())],
    )
    return pl.pallas_call(
        _dispatch_kernel,
        grid_spec=grid_spec,
        out_shape=jax.ShapeDtypeStruct(xbuf0.shape, xbuf0.dtype),
        input_output_aliases={2: 0},
        compiler_params=pltpu.CompilerParams(dimension_semantics=("arbitrary",), has_side_effects=True),
        name="moe_dispatch",
    )(dest_flat, h2, xbuf0)


def _expert_kernel(ce_ref, used_ref, x_ref, wg_ref, wu_ref, wd_ref, y_ref):
    c = pl.program_id(0)

    @pl.when(c < used_ref[0])
    def _():
        x = x_ref[...].astype(BF16)
        g = jnp.dot(x, wg_ref[0].astype(BF16), preferred_element_type=F32)
        u = jnp.dot(x, wu_ref[0].astype(BF16), preferred_element_type=F32)
        hmid = (g * jax.nn.sigmoid(g) * u).astype(BF16)
        y_ref[...] = jnp.dot(hmid, wd_ref[0].astype(BF16), preferred_element_type=F32)

    @pl.when(c >= used_ref[0])
    def _():
        y_ref[...] = jnp.zeros_like(y_ref)


def _experts(chunk_e, n_used, xbuf, w_gate, w_up, w_down):
    P = xbuf.shape[0]
    n_chunks = P // MOE_CHUNK
    row_blk = lambda c, ce, nu: (jnp.minimum(c, nu[0] - 1), 0)
    wt_blk = lambda c, ce, nu: (ce[jnp.minimum(c, nu[0] - 1)], 0, 0)
    grid_spec = pltpu.PrefetchScalarGridSpec(
        num_scalar_prefetch=2,
        grid=(n_chunks,),
        in_specs=[
            pl.BlockSpec((MOE_CHUNK, D_MODEL), row_blk),
            pl.BlockSpec((1, D_MODEL, D_EXPERT), wt_blk),
            pl.BlockSpec((1, D_MODEL, D_EXPERT), wt_blk),
            pl.BlockSpec((1, D_EXPERT, D_MODEL), wt_blk),
        ],
        out_specs=pl.BlockSpec((MOE_CHUNK, D_MODEL), lambda c, ce, nu: (c, 0)),
    )
    return pl.pallas_call(
        _expert_kernel,
        grid_spec=grid_spec,
        out_shape=jax.ShapeDtypeStruct((P, D_MODEL), F32),
        compiler_params=_cparams(1),
        name="moe_experts",
    )(chunk_e, n_used, xbuf, w_gate, w_up, w_down)


COMB_TM = 256


def _combine_kernel(final, dest_ref, x_ref, rw_ref, ln_ref, y_ref, o_ref, r0, r1, sem):
    i = pl.program_id(0)
    base = i * COMB_TM

    def copy(t, k):
        dst = (r0, r1)[k]
        return pltpu.make_async_copy(y_ref.at[pl.ds(dest_ref[2 * (base + t) + k], 1)], dst.at[pl.ds(t, 1)], sem)

    def start(t, c):
        copy(t, 0).start()
        copy(t, 1).start()
        return c

    def wait(t, c):
        copy(t, 0).wait()
        copy(t, 1).wait()
        return c

    lax.fori_loop(0, COMB_TM, start, 0)
    lax.fori_loop(0, COMB_TM, wait, 0)
    rw = rw_ref[...]
    x2 = x_ref[...] + (rw[:, 0:1] * r0[...] + rw[:, 1:2] * r1[...])
    if final:
        ms = jnp.mean(x2 * x2, axis=-1, keepdims=True)
        x2 = x2 * lax.rsqrt(ms + RMS_EPS) * ln_ref[...]
    o_ref[...] = x2


def _combine(dest_flat, x1, rw, ln_f, ybuf, final):
    S = x1.shape[0]
    assert S % COMB_TM == 0
    grid_spec = pltpu.PrefetchScalarGridSpec(
        num_scalar_prefetch=1,
        grid=(S // COMB_TM,),
        in_specs=[
            pl.BlockSpec((COMB_TM, D_MODEL), lambda i, d: (i, 0)),
            pl.BlockSpec((COMB_TM, 128), lambda i, d: (i, 0)),
            pl.BlockSpec((1, D_MODEL), lambda i, d: (0, 0)),
            pl.BlockSpec(memory_space=pl.ANY),
        ],
        out_specs=pl.BlockSpec((COMB_TM, D_MODEL), lambda i, d: (i, 0)),
        scratch_shapes=[pltpu.VMEM((COMB_TM, D_MODEL), F32), pltpu.VMEM((COMB_TM, D_MODEL), F32),
                        pltpu.SemaphoreType.DMA(())],
    )
    return pl.pallas_call(
        functools.partial(_combine_kernel, final),
        grid_spec=grid_spec,
        out_shape=jax.ShapeDtypeStruct((S, D_MODEL), F32),
        compiler_params=_cparams(1),
        name="moe_combine",
    )(dest_flat, x1, rw, ln_f, ybuf)


def _moe_plan(ri, counts):
    S = ri.shape[0]
    cnt = counts[0, ROUTER_LANE0:ROUTER_LANE0 + N_EXPERTS].astype(I32)
    padded = (cnt + MOE_CHUNK - 1) // MOE_CHUNK * MOE_CHUNK
    pend = jnp.cumsum(padded)
    off = pend - padded
    eid = ri[:, RT_E1:RT_E2 + 1]
    rank = ri[:, RT_R1:RT_R2 + 1]
    dest = (off[eid] + rank).reshape(-1)
    n_chunks = (S * 2) // MOE_CHUNK + N_EXPERTS
    chunk_e = jnp.minimum(jnp.searchsorted(pend, jnp.arange(n_chunks, dtype=I32) * MOE_CHUNK, side='right'),
                          N_EXPERTS - 1).astype(I32)
    n_used = (pend[-1:] // MOE_CHUNK).astype(I32)
    return dest, chunk_e, n_used, n_chunks


def _attention_block(x, l, ln1, w_in, diff_lambda, diff_subln, fox_bf, cmp_pos_k, cmp_w1_k, cmp_w2_k,
                     cmp_pos_v, cmp_w1_v, cmp_w2_v, bias_a, bias_c, overlap):
    S = x.shape[0]
    w_main = jnp.concatenate([w_in[:, :B_FOX_F], w_in[:, B_FOX_F + FOX_HEADS:B_NSA_G]], axis=1).astype(BF16)
    w_gate = jnp.concatenate([w_in[:, B_FOX_F:B_FOX_F + FOX_HEADS], w_in[:, B_NSA_G:]], axis=1)
    w_gate = jnp.pad(w_gate, ((0, 0), (0, GATE_COLS - w_gate.shape[1]))).astype(BF16)
    p64, gate = _inproj(x, ln1.reshape(1, D_MODEL), w_main, w_gate)

    v128 = jnp.transpose(p64[OFF_AV:OFF_AV + 2 * DIFF_HEADS].reshape(DIFF_HEADS, 2, S, HEAD_DIM),
                         (0, 2, 1, 3)).reshape(DIFF_HEADS, S, 2 * HEAD_DIM)
    lam_init = 0.8 - 0.6 * math.exp(-0.3 * l)
    cst = jnp.zeros((1, 128), F32).at[0, 0].set(lam_init)
    o_a = _diff_attention(p64, v128, bias_a, diff_lambda.astype(F32), diff_subln.reshape(1, -1), cst)

    bf_row = jnp.pad(fox_bf.astype(F32), (0, GATE_COLS - FOX_HEADS)).reshape(1, GATE_COLS)
    fox_qa, fox_ka = _fox_cumsum(gate, bf_row, p64)
    o_b = _fox_attention(fox_qa, fox_ka, p64)

    half = CMP_STRIDE * HEAD_DIM
    chunks = p64[OFF_CKC:OFF_CKC + 2 * NSA_GROUPS].reshape(2 * NSA_GROUPS, S // CMP_STRIDE, half)
    pos = jnp.stack([cmp_pos_k, cmp_pos_v]).astype(F32).reshape(2, 2, half)
    w1 = jnp.stack([cmp_w1_k, cmp_w1_v]).astype(BF16)
    w2 = jnp.stack([cmp_w2_k, cmp_w2_v]).astype(BF16)
    kvc = _compress(chunks, pos, w1, w2)
    o_c, q_aug, k_aug = _cmp_attention(p64, kvc, overlap)
    o_s = _sel_attention(q_aug, k_aug, p64, bias_c[0])
    o_w = _win_attention(p64, bias_c[1])
    return o_a, o_b, o_c, o_s, o_w, gate


def _layer(x, l, final, ln1, w_in, diff_lambda, diff_subln, fox_bf, cmp_pos_k, cmp_w1_k, cmp_w2_k,
           cmp_pos_v, cmp_w1_v, cmp_w2_v, bias_a, bias_c, overlap, w_out, ln2, wg, bg, we, be,
           w_gate, w_up, w_down, ln_f):
    S = x.shape[0]
    o_a, o_b, o_c, o_s, o_w, gate = _attention_block(
        x, l, ln1, w_in, diff_lambda, diff_subln, fox_bf, cmp_pos_k, cmp_w1_k, cmp_w2_k,
        cmp_pos_v, cmp_w1_v, cmp_w2_v, bias_a, bias_c, overlap)
    wr = jnp.pad(jnp.concatenate([wg, we], axis=1).astype(F32), ((0, 0), (0, 128 - N_GROUPS - N_EXPERTS)))
    br = jnp.pad(jnp.concatenate([bg, be]).astype(F32), (0, 128 - N_GROUPS - N_EXPERTS)).reshape(1, 128)
    x1, h2, ri, rw, counts = _outproj(o_a, o_b, o_c, o_s, o_w, gate, x, w_out.astype(BF16),
                                      ln2.reshape(1, D_MODEL), wr, br)
    dest, chunk_e, n_used, n_chunks = _moe_plan(ri, counts)
    xbuf = _dispatch(dest, h2, jnp.zeros((n_chunks * MOE_CHUNK, D_MODEL), F32))
    ybuf = _experts(chunk_e + l * N_EXPERTS, n_used, xbuf, w_gate, w_up, w_down)
    return _combine(dest, x1, rw, ln_f.reshape(1, D_MODEL), ybuf, final)


def _make_biases(t5_table, S):
    t5_a, t5_c = t5_table[:, :DIFF_HEADS], t5_table[:, DIFF_HEADS:]
    c_t = min(C_T, S)
    return (_bias_tiles(t5_a, min(A_TQ, S), min(A_TK, S)),
            (_bias_tiles(t5_c, min(SEL_TQ, S), min(SEL_TK, S)), _bias_tiles(t5_c, c_t, c_t)))


def kernel(x, ln1, w_in, diff_lambda, diff_subln, fox_bf, cmp_pos_k, cmp_w1_k, cmp_w2_k, cmp_pos_v, cmp_w1_v,
           cmp_w2_v, t5_table, w_out, ln2, router_group_w, router_group_b, router_expert_w, router_expert_b,
           w_gate, w_up, w_down, ln_f):
    B, S, _ = x.shape
    assert B == 1
    depth = w_in.shape[0]
    bias_a, bias_c = _make_biases(t5_table, S)
    overlap = _overlap_matrix(S // CMP_STRIDE)
    xs = x.reshape(S, D_MODEL)
    wg_all = w_gate.reshape(depth * N_EXPERTS, D_MODEL, D_EXPERT)
    wu_all = w_up.reshape(depth * N_EXPERTS, D_MODEL, D_EXPERT)
    wd_all = w_down.reshape(depth * N_EXPERTS, D_EXPERT, D_MODEL)
    for l in range(depth):
        xs = _layer(xs, l, l == depth - 1, ln1[l], w_in[l], diff_lambda[l], diff_subln[l], fox_bf[l],
                    cmp_pos_k[l], cmp_w1_k[l], cmp_w2_k[l], cmp_pos_v[l], cmp_w1_v[l], cmp_w2_v[l],
                    bias_a, bias_c, overlap, w_out[l], ln2[l], router_group_w[l], router_group_b[l],
                    router_expert_w[l], router_expert_b[l], wg_all, wu_all, wd_all, ln_f)
    return xs.reshape(B, S, D_MODEL)
```

```python
import functools
import math

import numpy as np
import jax
import jax.numpy as jnp
from jax import lax
from jax.experimental import pallas as pl
from jax.experimental.pallas import tpu as pltpu

F32 = jnp.float32
BF16 = jnp.bfloat16
I32 = jnp.int32

D_MODEL = 2048
HEAD_DIM = 64
DIFF_HEADS = 4
FOX_HEADS = 8
NSA_HEADS = 16
NSA_GROUPS = 4
NSA_HPG = 4
CMP_BLOCK = 32
CMP_STRIDE = 16
CMP_HIDDEN = 256
SEL_BLOCK = 64
SEL_TOPK = 16
WINDOW = 512
NUM_BUCKETS = 32
MAX_DISTANCE = 128
N_GROUPS = 8
EXPERTS_PER_GROUP = 8
N_EXPERTS = 64
D_EXPERT = 256
MOE_CHUNK = 256
RMS_EPS = 1e-6
NEG_INF = -1e30
SEL_FORCE = 1e4
SCALE = HEAD_DIM ** -0.5

N_MAIN = 5632
B_FOX_F = 3072
B_NSA_G = 5640
GATE_COLS = 128
SEL_LANES = 128
SEL_MASKED = -32768.0
PICKED = -3.0e38

OFF_AQ, OFF_AK, OFF_AV = 0, 8, 16
OFF_BQ, OFF_BK, OFF_BV = 24, 32, 40
OFF_CQ = 48
OFF_CKC, OFF_CVC, OFF_CKS, OFF_CVS, OFF_CKW, OFF_CVW = 64, 68, 72, 76, 80, 84

VMEM_LIMIT = 56 * 1024 * 1024


def _cparams(n_axes):
    return pltpu.CompilerParams(dimension_semantics=("arbitrary",) * n_axes,
                                vmem_limit_bytes=VMEM_LIMIT)


IN_TM = 1024
IN_TN = 512


def _inproj_kernel(x_ref, g_ref, w_ref, wg_ref, o64_ref, og_ref, h_ref):
    j = pl.program_id(1)

    @pl.when(j == 0)
    def _():
        x = x_ref[...]
        ms = jnp.mean(x * x, axis=-1, keepdims=True)
        h_ref[...] = (x * lax.rsqrt(ms + RMS_EPS) * g_ref[...]).astype(BF16)

    res = jnp.dot(h_ref[...], w_ref[...], preferred_element_type=F32)
    for b in range(IN_TN // HEAD_DIM):
        o64_ref[b] = res[:, b * HEAD_DIM:(b + 1) * HEAD_DIM].astype(BF16)

    @pl.when(j == pl.num_programs(1) - 1)
    def _():
        og_ref[...] = jnp.dot(h_ref[...], wg_ref[...], preferred_element_type=F32)


def _inproj(x, g, w_main, w_gate):
    S = x.shape[0]
    tm = min(IN_TM, S)
    nj = N_MAIN // IN_TN
    return pl.pallas_call(
        _inproj_kernel,
        grid=(S // tm, nj),
        in_specs=[
            pl.BlockSpec((tm, D_MODEL), lambda i, j: (i, 0)),
            pl.BlockSpec((1, D_MODEL), lambda i, j: (0, 0)),
            pl.BlockSpec((D_MODEL, IN_TN), lambda i, j: (0, j)),
            pl.BlockSpec((D_MODEL, GATE_COLS), lambda i, j: (0, 0)),
        ],
        out_specs=[
            pl.BlockSpec((IN_TN // HEAD_DIM, tm, HEAD_DIM), lambda i, j: (j, i, 0)),
            pl.BlockSpec((tm, GATE_COLS), lambda i, j: (i, 0)),
        ],
        out_shape=[
            jax.ShapeDtypeStruct((N_MAIN // HEAD_DIM, S, HEAD_DIM), BF16),
            jax.ShapeDtypeStruct((S, GATE_COLS), F32),
        ],
        scratch_shapes=[pltpu.VMEM((tm, D_MODEL), BF16)],
        compiler_params=_cparams(2),
        name="inproj",
    )(x, g, w_main, w_gate)


CUM_TB = 512


def _split3(x):
    hi = x.astype(BF16)
    r1 = x - hi.astype(F32)
    mid = r1.astype(BF16)
    lo = (r1 - mid.astype(F32)).astype(BF16)
    return hi, mid, lo


def _cumsum_kernel(gate_ref, bf_ref, q_ref, k_ref, qa_ref, ka_ref, carry_ref):
    i = pl.program_id(0)

    @pl.when(i == 0)
    def _():
        carry_ref[...] = jnp.zeros_like(carry_ref)

    z = gate_ref[...] + bf_ref[...]
    log_f = -(jnp.maximum(-z, 0.0) + jnp.log1p(jnp.exp(-jnp.abs(z))))
    tb = log_f.shape[0]
    row = lax.broadcasted_iota(I32, (tb, tb), 0)
    col = lax.broadcasted_iota(I32, (tb, tb), 1)
    tri = jnp.where(row >= col, 1.0, 0.0).astype(BF16)
    hi, mid, lo = _split3(log_f)
    c = (jnp.dot(tri, hi, preferred_element_type=F32)
         + jnp.dot(tri, mid, preferred_element_type=F32)
         + jnp.dot(tri, lo, preferred_element_type=F32)) + carry_ref[...]
    carry_ref[...] = c[tb - 1:tb, :]
    c2 = c * LOG2E
    lane = lax.broadcasted_iota(I32, (tb, HEAD_DIM), 1)
    for h in range(FOX_HEADS):
        parts = [t.astype(F32) for t in _split3(c2[:, h:h + 1])]
        tail_q = jnp.where((lane >= 3) & (lane < 6), 1.0, 0.0)
        tail_k = jnp.where(lane < 3, 1.0, 0.0)
        for j, t in enumerate(parts):
            tail_q = jnp.where(lane == j, t, tail_q)
            tail_k = jnp.where(lane == 3 + j, -t, tail_k)
        qa_ref[h, :, 0:HEAD_DIM] = q_ref[h] * (SCALE * LOG2E)
        qa_ref[h, :, HEAD_DIM:] = tail_q.astype(BF16)
        ka_ref[h, :, 0:HEAD_DIM] = k_ref[h]
        ka_ref[h, :, HEAD_DIM:] = tail_k.astype(BF16)


def _fox_cumsum(gate, bf_row, p64):
    S = gate.shape[0]
    tb = min(CUM_TB, S)
    assert OFF_BQ % FOX_HEADS == 0 and OFF_BK % FOX_HEADS == 0
    out = jax.ShapeDtypeStruct((FOX_HEADS, S, 2 * HEAD_DIM), BF16)
    return pl.pallas_call(
        _cumsum_kernel,
        grid=(S // tb,),
        in_specs=[pl.BlockSpec((tb, GATE_COLS), lambda i: (i, 0)),
                  pl.BlockSpec((1, GATE_COLS), lambda i: (0, 0)),
                  pl.BlockSpec((FOX_HEADS, tb, HEAD_DIM), lambda i: (OFF_BQ // FOX_HEADS, i, 0)),
                  pl.BlockSpec((FOX_HEADS, tb, HEAD_DIM), lambda i: (OFF_BK // FOX_HEADS, i, 0))],
        out_specs=[pl.BlockSpec((FOX_HEADS, tb, 2 * HEAD_DIM), lambda i: (0, i, 0)),
                   pl.BlockSpec((FOX_HEADS, tb, 2 * HEAD_DIM), lambda i: (0, i, 0))],
        out_shape=[out, out],
        scratch_shapes=[pltpu.VMEM((1, GATE_COLS), F32)],
        compiler_params=_cparams(1),
        name="fox_cumsum",
    )(gate, bf_row, p64, p64)


def _tile_pairs(S, tq, tk):
    assert tk % tq == 0
    qi_l, ki_l, cls_l, last_l = [], [], [], []
    for qi in range(S // tq):
        n_k = (qi * tq + tq - 1) // tk + 1
        for ki in range(n_k):
            qi_l.append(qi)
            ki_l.append(ki)
            cls_l.append((qi * tq - ki * tk) // tq)
            last_l.append(int(ki == n_k - 1))
    return tuple(jnp.asarray(np.asarray(a, np.int32)) for a in (qi_l, ki_l, cls_l, last_l))


def _n_bias_classes(tq, tk):
    return -(-(tk + MAX_DISTANCE) // tq)


STAT_LANES = 128
LOG2E = 1.4426950408889634


def _chunk_rows(tk):
    return 32 if tk <= 512 else 16


def _flash_chains(chains, s_ref, p_ref, alpha_ref, m_ref, l_ref, acc_ref):
    width = acc_ref.shape[1]

    def park(chain):
        scores_fn, _, row0, nrows, _ = chain
        s_ref[row0:row0 + nrows, :] = scores_fn()

    def softmax_pv(chain):
        _, v, row0, nrows, addend = chain
        tk = s_ref.shape[1]
        reps = tk // STAT_LANES
        rc = _chunk_rows(tk)

        def chunk(c):
            sl = slice(row0 + c * rc, row0 + (c + 1) * rc)
            s = s_ref[sl, :]
            return sl, (s if addend is None else s + addend(c * rc, rc))

        for c in range(nrows // rc):
            sl, s = chunk(c)
            m_prev = m_ref[sl, :]
            m_new = jnp.maximum(m_prev, jnp.max(s, axis=1, keepdims=True))
            alpha_ref[sl, :] = jnp.exp2(m_prev - m_new)
            m_ref[sl, :] = m_new
        for c in range(nrows // rc):
            sl, s = chunk(c)
            p = jnp.exp2(s - jnp.tile(m_ref[sl, :], (1, reps)))
            part = p[:, 0:STAT_LANES]
            for r in range(1, reps):
                part = part + p[:, r * STAT_LANES:(r + 1) * STAT_LANES]
            l_ref[sl, :] = alpha_ref[sl, :] * l_ref[sl, :] + part
            p_ref[sl, :] = p.astype(BF16)
        rows = slice(row0, row0 + nrows)
        acc_ref[rows, :] = (alpha_ref[rows, 0:width] * acc_ref[rows, :]
                            + jnp.dot(p_ref[rows, :], v, preferred_element_type=F32))

    park(chains[0])
    for i, chain in enumerate(chains):
        if i + 1 < len(chains):
            park(chains[i + 1])
        softmax_pv(chain)


def _row_sum(l):
    return jnp.broadcast_to(jnp.sum(l, axis=1, keepdims=True), l.shape)


def _init_state(m_ref, l_ref, acc_ref):
    m_ref[...] = jnp.full_like(m_ref, NEG_INF)
    l_ref[...] = jnp.zeros_like(l_ref)
    acc_ref[...] = jnp.zeros_like(acc_ref)


def _qk(q, k):
    return lax.dot_general(q, k, (((1,), (1,)), ((), ())), preferred_element_type=F32)


def _t5_bucket(rel):
    n = jnp.maximum(rel, 0)
    max_exact = NUM_BUCKETS // 2
    nf = jnp.maximum(n, max_exact).astype(F32)
    large = max_exact + (jnp.log(nf / max_exact) / math.log(MAX_DISTANCE / max_exact)
                         * (NUM_BUCKETS - max_exact)).astype(I32)
    return jnp.where(n < max_exact, n, jnp.minimum(large, NUM_BUCKETS - 1))


def _bias_tiles(table, tq, tk):
    n_cls = _n_bias_classes(tq, tk)
    i = jnp.arange(tq)[:, None]
    j = jnp.arange(tk)[None, :]
    rel = jnp.stack([c * tq + i - j for c in range(n_cls)])
    t = table.astype(F32)
    bucket = _t5_bucket(rel)[..., None]
    b = jnp.zeros(rel.shape + (t.shape[1],), F32)
    for k in range(NUM_BUCKETS - 1):
        b = jnp.where(bucket == k, (t[k] - t[NUM_BUCKETS - 1]) * LOG2E, b)
    b = jnp.where((rel >= 0)[..., None], b, NEG_INF)
    return jnp.transpose(b, (3, 0, 1, 2))


A_TQ = 1024
A_TK = 1024


def _diff_kernel(n_cls, qt_ref, kt_ref, cls_ref, last_ref, q1_ref, q2_ref, k1_ref, k2_ref, v_ref, bias_ref,
                 dl_ref, sub_ref, cst_ref, o_ref, qs, s_ref, p_ref, alpha, m, l, acc):
    p = pl.program_id(1)
    ki = kt_ref[p]
    T = q1_ref.shape[1]
    half = T // 2

    @pl.when(ki == 0)
    def _():
        _init_state(m, l, acc)
        qs[0:T, :] = q1_ref[0] * (SCALE * LOG2E)
        qs[T:2 * T, :] = q2_ref[0] * (SCALE * LOG2E)

    def step(with_bias):
        chains = []
        for j, k_ref in enumerate((k1_ref, k2_ref)):
            for hh in range(2):
                r0 = j * T + hh * half
                bias = ((lambda r, n, hh=hh: bias_ref[0, 0, hh * half + r:hh * half + r + n, :])
                        if with_bias else None)
                chains.append((lambda r0=r0, k_ref=k_ref: _qk(qs[r0:r0 + half, :], k_ref[0]),
                               v_ref[0], r0, half, bias))
        _flash_chains(chains, s_ref, p_ref, alpha, m, l, acc)

    @pl.when(cls_ref[p] < n_cls)
    def _():
        step(True)

    @pl.when(cls_ref[p] >= n_cls)
    def _():
        step(False)

    @pl.when(last_ref[p] == 1)
    def _():
        dl = dl_ref[...]
        lam_init = cst_ref[0:1, 0:1]
        lam = (jnp.exp(jnp.sum(dl[0:1] * dl[1:2], axis=1, keepdims=True))
               - jnp.exp(jnp.sum(dl[2:3] * dl[3:4], axis=1, keepdims=True)) + lam_init)
        o = acc[0:T, :] / _row_sum(l[0:T, :]) - lam * (acc[T:2 * T, :] / _row_sum(l[T:2 * T, :]))
        ms = jnp.mean(o * o, axis=-1, keepdims=True)
        y = o * lax.rsqrt(ms + RMS_EPS) * sub_ref[...]
        o_ref[0] = (y * (1.0 - lam_init)).astype(BF16)


def _diff_attention(p64, v128, bias, dl, subln, cst):
    S = p64.shape[1]
    T, TK = bias.shape[2], bias.shape[3]
    n_cls = bias.shape[1]
    tabs = _tile_pairs(S, T, TK)
    qblk = lambda off: pl.BlockSpec((1, T, HEAD_DIM), lambda h, p, qt, kt, cl, la: (off + 2 * h, qt[p], 0))
    kblk = lambda off: pl.BlockSpec((1, TK, HEAD_DIM), lambda h, p, qt, kt, cl, la: (off + 2 * h, kt[p], 0))
    const = lambda shape: pl.BlockSpec(shape, lambda h, p, qt, kt, cl, la: (0,) * len(shape))
    grid_spec = pltpu.PrefetchScalarGridSpec(
        num_scalar_prefetch=4,
        grid=(DIFF_HEADS, int(tabs[0].shape[0])),
        in_specs=[
            qblk(OFF_AQ), qblk(OFF_AQ + 1), kblk(OFF_AK), kblk(OFF_AK + 1),
            pl.BlockSpec((1, TK, 2 * HEAD_DIM), lambda h, p, qt, kt, cl, la: (h, kt[p], 0)),
            pl.BlockSpec((1, 1, T, TK), lambda h, p, qt, kt, cl, la: (h, jnp.minimum(cl[p], n_cls - 1), 0, 0)),
            const((4, HEAD_DIM)), const((1, 2 * HEAD_DIM)), const((1, 128)),
        ],
        out_specs=pl.BlockSpec((1, T, 2 * HEAD_DIM), lambda h, p, qt, kt, cl, la: (h, qt[p], 0)),
        scratch_shapes=[
            pltpu.VMEM((2 * T, HEAD_DIM), BF16),
            pltpu.VMEM((2 * T, TK), F32), pltpu.VMEM((2 * T, TK), BF16), pltpu.VMEM((2 * T, STAT_LANES), F32),
            pltpu.VMEM((2 * T, STAT_LANES), F32), pltpu.VMEM((2 * T, STAT_LANES), F32),
            pltpu.VMEM((2 * T, 2 * HEAD_DIM), F32),
        ],
    )
    return pl.pallas_call(
        functools.partial(_diff_kernel, n_cls),
        grid_spec=grid_spec,
        out_shape=jax.ShapeDtypeStruct((DIFF_HEADS, S, 2 * HEAD_DIM), BF16),
        compiler_params=_cparams(2),
        name="diff_attn",
    )(*tabs, p64, p64, p64, p64, v128, bias, dl, subln, cst)


B_TQ = 1024
B_TK = 1024


def _fox_kernel(diag_cls, qt_ref, kt_ref, cls_ref, last_ref, q_ref, k_ref, v_ref, o_ref,
                s_ref, p_ref, alpha, m, l, acc):
    p = pl.program_id(1)
    T = q_ref.shape[1]
    TK = k_ref.shape[1]
    half = T // 2

    @pl.when(kt_ref[p] == 0)
    def _():
        _init_state(m, l, acc)

    def step(diag):
        lead = cls_ref[p] * T
        chains = []
        for hh in range(2):
            r0 = hh * half

            def causal(r, n, r0=r0):
                row = lead + (r0 + r) + lax.broadcasted_iota(I32, (n, TK), 0)
                col = lax.broadcasted_iota(I32, (n, TK), 1)
                return jnp.where(row >= col, 0.0, NEG_INF)

            chains.append((lambda r0=r0: _qk(q_ref[0, r0:r0 + half, :], k_ref[0]), v_ref[0], r0, half,
                           causal if diag else None))
        _flash_chains(chains, s_ref, p_ref, alpha, m, l, acc)

    @pl.when(cls_ref[p] < diag_cls)
    def _():
        step(True)

    @pl.when(cls_ref[p] >= diag_cls)
    def _():
        step(False)

    @pl.when(last_ref[p] == 1)
    def _():
        o_ref[0] = (acc[...] / _row_sum(l[...])[:, 0:HEAD_DIM]).astype(BF16)


def _fox_attention(qa, ka, p64):
    S = p64.shape[1]
    T = min(B_TQ, S)
    TK = min(B_TK, S)
    aug = qa.shape[2]
    tabs = _tile_pairs(S, T, TK)
    grid_spec = pltpu.PrefetchScalarGridSpec(
        num_scalar_prefetch=4,
        grid=(FOX_HEADS, int(tabs[0].shape[0])),
        in_specs=[
            pl.BlockSpec((1, T, aug), lambda h, p, qt, kt, cl, la: (h, qt[p], 0)),
            pl.BlockSpec((1, TK, aug), lambda h, p, qt, kt, cl, la: (h, kt[p], 0)),
            pl.BlockSpec((1, TK, HEAD_DIM), lambda h, p, qt, kt, cl, la: (OFF_BV + h, kt[p], 0)),
        ],
        out_specs=pl.BlockSpec((1, T, HEAD_DIM), lambda h, p, qt, kt, cl, la: (h, qt[p], 0)),
        scratch_shapes=[
            pltpu.VMEM((T, TK), F32), pltpu.VMEM((T, TK), BF16), pltpu.VMEM((T, STAT_LANES), F32),
            pltpu.VMEM((T, STAT_LANES), F32), pltpu.VMEM((T, STAT_LANES), F32), pltpu.VMEM((T, HEAD_DIM), F32),
        ],
    )
    return pl.pallas_call(
        functools.partial(_fox_kernel, TK // T),
        grid_spec=grid_spec,
        out_shape=jax.ShapeDtypeStruct((FOX_HEADS, S, HEAD_DIM), BF16),
        compiler_params=_cparams(2),
        name="fox_attn",
    )(*tabs, qa, ka, p64)


def _gelu_tanh(x):
    return 0.5 * x * (1.0 + jnp.tanh(math.sqrt(2.0 / math.pi) * (x + 0.044715 * (x * x * x))))


def _compress_kernel(x_ref, pos_ref, w1_ref, w2_ref, o_ref):
    x = x_ref[0].astype(F32)
    half = x.shape[1]
    pos = pos_ref[0]
    x_lo = (x + pos[0:1]).astype(BF16)
    x_hi = (x + pos[1:2]).astype(BF16)
    w1 = w1_ref[0]
    y_lo = jnp.dot(x_lo, w1[:half], preferred_element_type=F32)
    y_hi = jnp.dot(x_hi, w1[half:], preferred_element_type=F32)
    n = y_hi.shape[0]
    pre = y_lo + pltpu.roll(y_hi, n - 1, 0)
    o_ref[0] = jnp.dot(_gelu_tanh(pre).astype(BF16), w2_ref[0], preferred_element_type=F32).astype(BF16)


def _compress(chunks, pos, w1, w2):
    n = chunks.shape[1]
    half = CMP_STRIDE * HEAD_DIM
    return pl.pallas_call(
        _compress_kernel,
        grid=(2 * NSA_GROUPS,),
        in_specs=[
            pl.BlockSpec((1, n, half), lambda i: (i, 0, 0)),
            pl.BlockSpec((1, 2, half), lambda i: (i // NSA_GROUPS, 0, 0)),
            pl.BlockSpec((1, 2 * half, CMP_HIDDEN), lambda i: (i // NSA_GROUPS, 0, 0)),
            pl.BlockSpec((1, CMP_HIDDEN, HEAD_DIM), lambda i: (i // NSA_GROUPS, 0, 0)),
        ],
        out_specs=pl.BlockSpec((1, n, HEAD_DIM), lambda i: (i, 0, 0)),
        out_shape=jax.ShapeDtypeStruct((2 * NSA_GROUPS, n, HEAD_DIM), BF16),
        compiler_params=_cparams(1),
        name="nsa_compress",
    )(chunks, pos, w1, w2)


C_T = 256


def _cmp_kernel(q_ref, kc_ref, vc_ref, ks_ref, ov_ref, oc_ref, qa_ref, ka_ref):
    qi = pl.program_id(1)
    T = q_ref.shape[1]
    ncp = kc_ref.shape[1]
    q0 = qi * T
    qs = (q_ref[...] * SCALE).reshape(NSA_HPG * T, HEAD_DIM)
    s = _qk(qs, kc_ref[0]).reshape(NSA_HPG, T, ncp)
    t = q0 + lax.broadcasted_iota(I32, (T, ncp), 0)
    n_idx = lax.broadcasted_iota(I32, (T, ncp), 1)
    visible = (CMP_STRIDE * n_idx + (CMP_BLOCK - 1) <= t)[None]
    s = jnp.where(visible, s, NEG_INF)
    smax = jnp.max(s, axis=2, keepdims=True)
    e = jnp.where(visible, jnp.exp(s - smax), 0.0)
    den = jnp.sum(e, axis=2, keepdims=True)
    p = e / jnp.where(den > 0.0, den, 1.0)
    o = jnp.dot(p.reshape(NSA_HPG * T, ncp).astype(BF16), vc_ref[0], preferred_element_type=F32)
    oc_ref[...] = o.reshape(NSA_HPG, T, HEAD_DIM).astype(BF16)

    psum = p[0] + p[1] + p[2] + p[3]
    hi, mid, lo = _split3(psum)
    ov = ov_ref[...]
    imp = (jnp.dot(hi, ov, preferred_element_type=F32) + jnp.dot(mid, ov, preferred_element_type=F32)
           + jnp.dot(lo, ov, preferred_element_type=F32))
    tq = q0 + lax.broadcasted_iota(I32, (T, SEL_LANES), 0)
    jb = lax.broadcasted_iota(I32, (T, SEL_LANES), 1)
    cur = tq // SEL_BLOCK
    forced = (jb == 0) | (jb == cur) | (jb == cur - 1)
    valid = SEL_BLOCK * jb <= tq
    score = jnp.where(valid, imp + jnp.where(forced, SEL_FORCE, 0.0), NEG_INF)
    sc = score.T
    jrow = lax.broadcasted_iota(I32, sc.shape, 0).astype(F32)
    sel = jnp.zeros(sc.shape, F32)
    for _ in range(SEL_TOPK):
        mx = jnp.max(sc, axis=0, keepdims=True)
        first = jnp.min(jnp.where(sc == mx, jrow, float(SEL_LANES)), axis=0, keepdims=True)
        pick = jrow == first
        sel = jnp.where(pick, 1.0, sel)
        sc = jnp.where(pick, PICKED, sc)
    sel_neg = jnp.where(sel.T > 0.5, 0.0, SEL_MASKED).astype(BF16)

    zeros = jnp.zeros((T, HEAD_DIM), BF16)
    for hh in range(NSA_HPG):
        qa_ref[hh, :, 0:HEAD_DIM] = q_ref[hh] * (SCALE * LOG2E)
        qa_ref[hh, :, HEAD_DIM:2 * HEAD_DIM] = zeros
        qa_ref[hh, :, 2 * HEAD_DIM:] = sel_neg
    kblk = (q0 + lax.broadcasted_iota(I32, (T, SEL_LANES), 0)) // SEL_BLOCK
    onehot = jnp.where(kblk == jb, 1.0, 0.0).astype(BF16)
    ka_ref[0, :, 0:HEAD_DIM] = ks_ref[0]
    ka_ref[0, :, HEAD_DIM:2 * HEAD_DIM] = zeros
    ka_ref[0, :, 2 * HEAD_DIM:] = onehot


def _cmp_attention(p64, kvc, overlap):
    S = p64.shape[1]
    T = min(C_T, S)
    ncp = kvc.shape[1]
    aug = 2 * HEAD_DIM + SEL_LANES
    return pl.pallas_call(
        _cmp_kernel,
        grid=(NSA_GROUPS, S // T),
        in_specs=[
            pl.BlockSpec((NSA_HPG, T, HEAD_DIM), lambda g, i: (OFF_CQ // NSA_HPG + g, i, 0)),
            pl.BlockSpec((1, ncp, HEAD_DIM), lambda g, i: (g, 0, 0)),
            pl.BlockSpec((1, ncp, HEAD_DIM), lambda g, i: (NSA_GROUPS + g, 0, 0)),
            pl.BlockSpec((1, T, HEAD_DIM), lambda g, i: (OFF_CKS + g, i, 0)),
            pl.BlockSpec((ncp, SEL_LANES), lambda g, i: (0, 0)),
        ],
        out_specs=[
            pl.BlockSpec((NSA_HPG, T, HEAD_DIM), lambda g, i: (g, i, 0)),
            pl.BlockSpec((NSA_HPG, T, aug), lambda g, i: (g, i, 0)),
            pl.BlockSpec((1, T, aug), lambda g, i: (g, i, 0)),
        ],
        out_shape=[
            jax.ShapeDtypeStruct((NSA_HEADS, S, HEAD_DIM), BF16),
            jax.ShapeDtypeStruct((NSA_HEADS, S, aug), BF16),
            jax.ShapeDtypeStruct((NSA_GROUPS, S, aug), BF16),
        ],
        compiler_params=_cparams(2),
        name="nsa_cmp_attn",
    )(p64, kvc, kvc, p64, overlap)


def _overlap_matrix(ncp):
    n = np.arange(ncp)[:, None] * CMP_STRIDE
    j = np.arange(SEL_LANES)[None, :] * SEL_BLOCK
    ov = np.clip(np.minimum(n + CMP_BLOCK, j + SEL_BLOCK) - np.maximum(n, j), 0, None) / CMP_BLOCK
    ov[ncp - 1:] = 0.0
    return jnp.asarray(ov, BF16)


SEL_TQ = 512
SEL_TK = 512


def _sel_kernel(n_cls, qt_ref, kt_ref, cls_ref, last_ref, qa_ref, ka_ref, v_ref, bias_ref, o_ref,
                s_ref, p_ref, alpha, m, l, acc):
    p = pl.program_id(1)
    T = qa_ref.shape[1]

    @pl.when(kt_ref[p] == 0)
    def _():
        _init_state(m, l, acc)

    def step(with_bias):
        chains = []
        for h in range(NSA_HPG):
            bias = (lambda r, n, h=h: bias_ref[h, 0, r:r + n, :]) if with_bias else None
            chains.append((lambda h=h: _qk(qa_ref[h], ka_ref[0]), v_ref[0], h * T, T, bias))
        _flash_chains(chains, s_ref, p_ref, alpha, m, l, acc)

    @pl.when(cls_ref[p] < n_cls)
    def _():
        step(True)

    @pl.when(cls_ref[p] >= n_cls)
    def _():
        step(False)

    @pl.when(last_ref[p] == 1)
    def _():
        o_ref[...] = (acc[...] / _row_sum(l[...])[:, :HEAD_DIM]).reshape(NSA_HPG, T, HEAD_DIM).astype(BF16)


def _sel_attention(q_aug, k_aug, p64, bias):
    S = p64.shape[1]
    n_cls, T, TK = bias.shape[1:]
    aug = q_aug.shape[2]
    tabs = _tile_pairs(S, T, TK)
    grid_spec = pltpu.PrefetchScalarGridSpec(
        num_scalar_prefetch=4,
        grid=(NSA_GROUPS, int(tabs[0].shape[0])),
        in_specs=[
            pl.BlockSpec((NSA_HPG, T, aug), lambda g, p, qt, kt, cl, la: (g, qt[p], 0)),
            pl.BlockSpec((1, TK, aug), lambda g, p, qt, kt, cl, la: (g, kt[p], 0)),
            pl.BlockSpec((1, TK, HEAD_DIM), lambda g, p, qt, kt, cl, la: (OFF_CVS + g, kt[p], 0)),
            pl.BlockSpec((NSA_HPG, 1, T, TK), lambda g, p, qt, kt, cl, la: (g, jnp.minimum(cl[p], n_cls - 1), 0, 0)),
        ],
        out_specs=pl.BlockSpec((NSA_HPG, T, HEAD_DIM), lambda g, p, qt, kt, cl, la: (g, qt[p], 0)),
        scratch_shapes=[pltpu.VMEM((NSA_HPG * T, TK), F32), pltpu.VMEM((NSA_HPG * T, TK), BF16),
                        pltpu.VMEM((NSA_HPG * T, STAT_LANES), F32),
                        pltpu.VMEM((NSA_HPG * T, STAT_LANES), F32), pltpu.VMEM((NSA_HPG * T, STAT_LANES), F32),
                        pltpu.VMEM((NSA_HPG * T, HEAD_DIM), F32)],
    )
    return pl.pallas_call(
        functools.partial(_sel_kernel, n_cls),
        grid_spec=grid_spec,
        out_shape=jax.ShapeDtypeStruct((NSA_HEADS, S, HEAD_DIM), BF16),
        compiler_params=_cparams(2),
        name="nsa_sel_attn",
    )(*tabs, q_aug, k_aug, p64, bias)


def _win_kernel(q_ref, k2_ref, k1_ref, k0_ref, v2_ref, v1_ref, v0_ref, bias_ref, o_ref,
                s_ref, p_ref, alpha, m, l, acc):
    qi = pl.program_id(1)
    T = q_ref.shape[1]
    _init_state(m, l, acc)
    v = jnp.concatenate([v2_ref[0], v1_ref[0], v0_ref[0]], axis=0)
    first_col = (2 - jnp.minimum(qi, 2)) * T

    def step(edge):
        chains = []
        for h in range(NSA_HPG):
            def bias(r, n, h=h):
                b = bias_ref[h, r:r + n, :]
                if edge:
                    col = lax.broadcasted_iota(I32, b.shape, 1)
                    b = jnp.where(col >= first_col, b, NEG_INF)
                return b

            def scores(h=h):
                q = q_ref[h] * (SCALE * LOG2E)
                return jnp.concatenate([_qk(q, k2_ref[0]), _qk(q, k1_ref[0]), _qk(q, k0_ref[0])], axis=1)

            chains.append((scores, v, h * T, T, bias))
        _flash_chains(chains, s_ref, p_ref, alpha, m, l, acc)

    @pl.when(qi >= 2)
    def _():
        step(False)

    @pl.when(qi < 2)
    def _():
        step(True)

    o_ref[...] = (acc[...] / _row_sum(l[...])[:, 0:HEAD_DIM]).reshape(NSA_HPG, T, HEAD_DIM).astype(BF16)


def _win_band_bias(bias):
    T = bias.shape[2]
    i = jnp.arange(T)[:, None]
    j = jnp.arange(T)[None, :]
    far = jnp.broadcast_to(jnp.where(j > i, 0.0, NEG_INF).astype(F32), (bias.shape[0], T, T))
    return jnp.concatenate([far, bias[:, 1], bias[:, 0]], axis=-1)


def _win_attention(p64, bias):
    S = p64.shape[1]
    T = bias.shape[1]
    assert WINDOW == 2 * T and bias.shape[2] == 3 * T
    kv = lambda off, back: pl.BlockSpec((1, T, HEAD_DIM), lambda g, i: (off + g, jnp.maximum(i - back, 0), 0))
    return pl.pallas_call(
        _win_kernel,
        grid=(NSA_GROUPS, S // T),
        in_specs=[
            pl.BlockSpec((NSA_HPG, T, HEAD_DIM), lambda g, i: (OFF_CQ // NSA_HPG + g, i, 0)),
            kv(OFF_CKW, 2), kv(OFF_CKW, 1), kv(OFF_CKW, 0), kv(OFF_CVW, 2), kv(OFF_CVW, 1), kv(OFF_CVW, 0),
            pl.BlockSpec((NSA_HPG, T, 3 * T), lambda g, i: (g, 0, 0)),
        ],
        out_specs=pl.BlockSpec((NSA_HPG, T, HEAD_DIM), lambda g, i: (g, i, 0)),
        out_shape=jax.ShapeDtypeStruct((NSA_HEADS, S, HEAD_DIM), BF16),
        scratch_shapes=[pltpu.VMEM((NSA_HPG * T, 3 * T), F32), pltpu.VMEM((NSA_HPG * T, 3 * T), BF16),
                        pltpu.VMEM((NSA_HPG * T, STAT_LANES), F32),
                        pltpu.VMEM((NSA_HPG * T, STAT_LANES), F32), pltpu.VMEM((NSA_HPG * T, STAT_LANES), F32),
                        pltpu.VMEM((NSA_HPG * T, HEAD_DIM), F32)],
        compiler_params=_cparams(2),
        name="nsa_win_attn",
    )(p64, p64, p64, p64, p64, p64, p64, bias)


OUT_TM = 256
RT_E1, RT_E2, RT_R1, RT_R2 = 0, 1, 2, 3
ROUTER_LANE0 = N_GROUPS


def _outproj_kernel(oa_ref, ob_ref, oc_ref, os_ref, ow_ref, gate_ref, x_ref, w_ref, ln_ref, wr_ref, br_ref,
                    x1_ref, h2_ref, ri_ref, rw_ref, cnt_ref, mix_ref, carry_ref):
    i = pl.program_id(0)
    tm = x_ref.shape[0]

    @pl.when(i == 0)
    def _():
        carry_ref[...] = jnp.zeros_like(carry_ref)

    for h in range(DIFF_HEADS):
        mix_ref[:, 128 * h:128 * (h + 1)] = oa_ref[h]
    for h in range(FOX_HEADS):
        c0 = 512 + HEAD_DIM * h
        mix_ref[:, c0:c0 + HEAD_DIM] = ob_ref[h]
    sig = jax.nn.sigmoid(gate_ref[...])
    for h in range(NSA_HEADS):
        c0 = 1024 + HEAD_DIM * h
        g0 = sig[:, FOX_HEADS + h:FOX_HEADS + h + 1]
        g1 = sig[:, FOX_HEADS + NSA_HEADS + h:FOX_HEADS + NSA_HEADS + h + 1]
        g2 = sig[:, FOX_HEADS + 2 * NSA_HEADS + h:FOX_HEADS + 2 * NSA_HEADS + h + 1]
        o = g0 * oc_ref[h].astype(F32) + g1 * os_ref[h].astype(F32) + g2 * ow_ref[h].astype(F32)
        mix_ref[:, c0:c0 + HEAD_DIM] = o.astype(BF16)

    x1 = x_ref[...] + jnp.dot(mix_ref[...], w_ref[...], preferred_element_type=F32)
    x1_ref[...] = x1
    ms = jnp.mean(x1 * x1, axis=-1, keepdims=True)
    h2 = x1 * lax.rsqrt(ms + RMS_EPS) * ln_ref[...]
    h2_ref[...] = h2

    a0, a1, _ = _split3(h2)
    b0, b1, _ = _split3(wr_ref[...])
    dot = lambda a, b: jnp.dot(a, b, preferred_element_type=F32)
    logits = dot(a0, b0) + (dot(a0, b1) + dot(a1, b0)) + br_ref[...]
    lane = lax.broadcasted_iota(I32, logits.shape, 1)
    lane_f = lane.astype(F32)

    def first_lane(cond):
        return jnp.min(jnp.where(cond, lane_f, 128.0), axis=1, keepdims=True).astype(I32)

    is_g = lane < N_GROUPS
    lg = jnp.where(is_g, logits, NEG_INF)
    gmax = jnp.max(lg, axis=1, keepdims=True)
    gsum = jnp.sum(jnp.where(is_g, jnp.exp(lg - gmax), 0.0), axis=1, keepdims=True)
    g_p = 1.0 / gsum
    g_idx = first_lane(lg == gmax)
    e_lane = lane - ROUTER_LANE0
    in_group = (e_lane >= 0) & (e_lane < N_EXPERTS) & ((e_lane >> 3) == g_idx)
    em = jnp.where(in_group, logits, NEG_INF)
    v1 = jnp.max(em, axis=1, keepdims=True)
    i1 = first_lane(em == v1)
    em2 = jnp.where(lane == i1, PICKED, em)
    v2 = jnp.max(em2, axis=1, keepdims=True)
    i2 = first_lane(em2 == v2)
    tt = jnp.exp(v2 - v1)
    w1 = g_p / (1.0 + tt)
    w2 = g_p * tt / (1.0 + tt)

    oh = jnp.where((lane == i1) | (lane == i2), 1.0, 0.0)
    row = lax.broadcasted_iota(I32, (tm, tm), 0)
    col = lax.broadcasted_iota(I32, (tm, tm), 1)
    tri = jnp.where(row > col, 1.0, 0.0).astype(BF16)
    prefix = jnp.dot(tri, oh.astype(BF16), preferred_element_type=F32) + carry_ref[...]
    r1 = jnp.sum(jnp.where(lane == i1, prefix, 0.0), axis=1, keepdims=True)
    r2 = jnp.sum(jnp.where(lane == i2, prefix, 0.0), axis=1, keepdims=True)
    carry = carry_ref[...] + jnp.sum(oh, axis=0, keepdims=True)
    carry_ref[...] = carry
    cnt_ref[...] = carry

    ri = jnp.where(lane == RT_E1, i1 - ROUTER_LANE0,
                   jnp.where(lane == RT_E2, i2 - ROUTER_LANE0,
                             jnp.where(lane == RT_R1, r1.astype(I32),
                                       jnp.where(lane == RT_R2, r2.astype(I32), 0))))
    ri_ref[...] = ri
    rw_ref[...] = jnp.where(lane == 0, w1, jnp.where(lane == 1, w2, 0.0))


def _outproj(o_a, o_b, o_c, o_s, o_w, gate, x, w_out, ln2, wr, br):
    S = x.shape[0]
    tm = min(OUT_TM, S)
    full = lambda shape: pl.BlockSpec(shape, lambda i: (0,) * len(shape))
    return pl.pallas_call(
        _outproj_kernel,
        grid=(S // tm,),
        in_specs=[
            pl.BlockSpec((DIFF_HEADS, tm, 2 * HEAD_DIM), lambda i: (0, i, 0)),
            pl.BlockSpec((FOX_HEADS, tm, HEAD_DIM), lambda i: (0, i, 0)),
            pl.BlockSpec((NSA_HEADS, tm, HEAD_DIM), lambda i: (0, i, 0)),
            pl.BlockSpec((NSA_HEADS, tm, HEAD_DIM), lambda i: (0, i, 0)),
            pl.BlockSpec((NSA_HEADS, tm, HEAD_DIM), lambda i: (0, i, 0)),
            pl.BlockSpec((tm, GATE_COLS), lambda i: (i, 0)),
            pl.BlockSpec((tm, D_MODEL), lambda i: (i, 0)),
            full((D_MODEL, D_MODEL)),
            full((1, D_MODEL)),
            full((D_MODEL, 128)),
            full((1, 128)),
        ],
        out_specs=[
            pl.BlockSpec((tm, D_MODEL), lambda i: (i, 0)),
            pl.BlockSpec((tm, D_MODEL), lambda i: (i, 0)),
            pl.BlockSpec((tm, 128), lambda i: (i, 0)),
            pl.BlockSpec((tm, 128), lambda i: (i, 0)),
            pl.BlockSpec((1, 128), lambda i: (0, 0)),
        ],
        out_shape=[
            jax.ShapeDtypeStruct((S, D_MODEL), F32),
            jax.ShapeDtypeStruct((S, D_MODEL), F32),
            jax.ShapeDtypeStruct((S, 128), I32),
            jax.ShapeDtypeStruct((S, 128), F32),
            jax.ShapeDtypeStruct((1, 128), F32),
        ],
        scratch_shapes=[pltpu.VMEM((tm, D_MODEL), BF16), pltpu.VMEM((1, 128), F32)],
        compiler_params=_cparams(1),
        name="outproj_router",
    )(o_a, o_b, o_c, o_s, o_w, gate, x, w_out, ln2, wr, br)


DISP_TM = 256
ROW_UNROLL = 8


def _dispatch_kernel(dest_ref, h_ref, xb_in_ref, xb_ref, sem):
    del xb_in_ref
    i = pl.program_id(0)
    base = i * DISP_TM

    def copy(t, k):
        return pltpu.make_async_copy(h_ref.at[pl.ds(t, 1)],
                                     xb_ref.at[pl.ds(dest_ref[2 * (base + t) + k], 1)], sem)

    def start(g, c):
        for u in range(ROW_UNROLL):
            copy(g * ROW_UNROLL + u, 0).start()
            copy(g * ROW_UNROLL + u, 1).start()
        return c

    def wait(g, c):
        for _ in range(2 * ROW_UNROLL):
            pltpu.make_async_copy(h_ref.at[pl.ds(0, 1)], xb_ref.at[pl.ds(0, 1)], sem).wait()
        return c

    lax.fori_loop(0, DISP_TM // ROW_UNROLL, start, 0)
    lax.fori_loop(0, DISP_TM // ROW_UNROLL, wait, 0)


def _dispatch(dest_flat, h2, xbuf0):
    S = h2.shape[0]
    assert S % DISP_TM == 0
    grid_spec = pltpu.PrefetchScalarGridSpec(
        num_scalar_prefetch=1,
        grid=(S // DISP_TM,),
        in_specs=[pl.BlockSpec((DISP_TM, D_MODEL), lambda i, d: (i, 0)), pl.BlockSpec(memory_space=pl.ANY)],
        out_specs=pl.BlockSpec(memory_space=pl.ANY),
        scratch_shapes=[pltpu.SemaphoreType.DMA(())],
    )
    return pl.pallas_call(
        _dispatch_kernel,
        grid_spec=grid_spec,
        out_shape=jax.ShapeDtypeStruct(xbuf0.shape, xbuf0.dtype),
        input_output_aliases={2: 0},
        compiler_params=pltpu.CompilerParams(dimension_semantics=("arbitrary",), has_side_effects=True),
        name="moe_dispatch",
    )(dest_flat, h2, xbuf0)


def _expert_kernel(ce_ref, used_ref, x_ref, wg_ref, wu_ref, wd_ref, y_ref):
    c = pl.program_id(0)

    @pl.when(c < used_ref[0])
    def _():
        x = x_ref[...].astype(BF16)
        g = jnp.dot(x, wg_ref[0].astype(BF16), preferred_element_type=F32)
        u = jnp.dot(x, wu_ref[0].astype(BF16), preferred_element_type=F32)
        hmid = (g * jax.nn.sigmoid(g) * u).astype(BF16)
        y_ref[...] = jnp.dot(hmid, wd_ref[0].astype(BF16), preferred_element_type=F32)

    @pl.when(c >= used_ref[0])
    def _():
        y_ref[...] = jnp.zeros_like(y_ref)


def _experts(chunk_e, n_used, xbuf, w_gate, w_up, w_down):
    P = xbuf.shape[0]
    n_chunks = P // MOE_CHUNK
    row_blk = lambda c, ce, nu: (jnp.minimum(c, nu[0] - 1), 0)
    wt_blk = lambda c, ce, nu: (ce[jnp.minimum(c, nu[0] - 1)], 0, 0)
    grid_spec = pltpu.PrefetchScalarGridSpec(
        num_scalar_prefetch=2,
        grid=(n_chunks,),
        in_specs=[
            pl.BlockSpec((MOE_CHUNK, D_MODEL), row_blk),
            pl.BlockSpec((1, D_MODEL, D_EXPERT), wt_blk),
            pl.BlockSpec((1, D_MODEL, D_EXPERT), wt_blk),
            pl.BlockSpec((1, D_EXPERT, D_MODEL), wt_blk),
        ],
        out_specs=pl.BlockSpec((MOE_CHUNK, D_MODEL), lambda c, ce, nu: (c, 0)),
    )
    return pl.pallas_call(
        _expert_kernel,
        grid_spec=grid_spec,
        out_shape=jax.ShapeDtypeStruct((P, D_MODEL), F32),
        compiler_params=_cparams(1),
        name="moe_experts",
    )(chunk_e, n_used, xbuf, w_gate, w_up, w_down)


COMB_TM = 256


def _combine_kernel(final, dest_ref, x_ref, rw_ref, ln_ref, y_ref, o_ref, r0, r1, sem):
    i = pl.program_id(0)
    base = i * COMB_TM

    def copy(t, k):
        dst = (r0, r1)[k]
        return pltpu.make_async_copy(y_ref.at[pl.ds(dest_ref[2 * (base + t) + k], 1)], dst.at[pl.ds(t, 1)], sem)

    def start(g, c):
        for u in range(ROW_UNROLL):
            copy(g * ROW_UNROLL + u, 0).start()
            copy(g * ROW_UNROLL + u, 1).start()
        return c

    def wait(g, c):
        for _ in range(2 * ROW_UNROLL):
            pltpu.make_async_copy(y_ref.at[pl.ds(0, 1)], r0.at[pl.ds(0, 1)], sem).wait()
        return c

    lax.fori_loop(0, COMB_TM // ROW_UNROLL, start, 0)
    lax.fori_loop(0, COMB_TM // ROW_UNROLL, wait, 0)
    rw = rw_ref[...]
    x2 = x_ref[...] + (rw[:, 0:1] * r0[...] + rw[:, 1:2] * r1[...])
    if final:
        ms = jnp.mean(x2 * x2, axis=-1, keepdims=True)
        x2 = x2 * lax.rsqrt(ms + RMS_EPS) * ln_ref[...]
    o_ref[...] = x2


def _combine(dest_flat, x1, rw, ln_f, ybuf, final):
    S = x1.shape[0]
    assert S % COMB_TM == 0
    grid_spec = pltpu.PrefetchScalarGridSpec(
        num_scalar_prefetch=1,
        grid=(S // COMB_TM,),
        in_specs=[
            pl.BlockSpec((COMB_TM, D_MODEL), lambda i, d: (i, 0)),
            pl.BlockSpec((COMB_TM, 128), lambda i, d: (i, 0)),
            pl.BlockSpec((1, D_MODEL), lambda i, d: (0, 0)),
            pl.BlockSpec(memory_space=pl.ANY),
        ],
        out_specs=pl.BlockSpec((COMB_TM, D_MODEL), lambda i, d: (i, 0)),
        scratch_shapes=[pltpu.VMEM((COMB_TM, D_MODEL), F32), pltpu.VMEM((COMB_TM, D_MODEL), F32),
                        pltpu.SemaphoreType.DMA(())],
    )
    return pl.pallas_call(
        functools.partial(_combine_kernel, final),
        grid_spec=grid_spec,
        out_shape=jax.ShapeDtypeStruct((S, D_MODEL), F32),
        compiler_params=_cparams(1),
        name="moe_combine",
    )(dest_flat, x1, rw, ln_f, ybuf)


def _moe_plan(ri, counts):
    S = ri.shape[0]
    cnt = counts[0, ROUTER_LANE0:ROUTER_LANE0 + N_EXPERTS].astype(I32)
    padded = (cnt + MOE_CHUNK - 1) // MOE_CHUNK * MOE_CHUNK
    pend = jnp.cumsum(padded)
    off = pend - padded
    eid = ri[:, RT_E1:RT_E2 + 1]
    rank = ri[:, RT_R1:RT_R2 + 1]
    dest = (off[eid] + rank).reshape(-1)
    n_chunks = (S * 2) // MOE_CHUNK + N_EXPERTS
    chunk_e = jnp.minimum(jnp.searchsorted(pend, jnp.arange(n_chunks, dtype=I32) * MOE_CHUNK, side='right'),
                          N_EXPERTS - 1).astype(I32)
    n_used = (pend[-1:] // MOE_CHUNK).astype(I32)
    return dest, chunk_e, n_used, n_chunks


def _attention_block(x, l, ln1, w_in, diff_lambda, diff_subln, fox_bf, cmp_pos_k, cmp_w1_k, cmp_w2_k,
                     cmp_pos_v, cmp_w1_v, cmp_w2_v, bias_a, bias_c, overlap):
    S = x.shape[0]
    w_main = jnp.concatenate([w_in[:, :B_FOX_F], w_in[:, B_FOX_F + FOX_HEADS:B_NSA_G]], axis=1).astype(BF16)
    w_gate = jnp.concatenate([w_in[:, B_FOX_F:B_FOX_F + FOX_HEADS], w_in[:, B_NSA_G:]], axis=1)
    w_gate = jnp.pad(w_gate, ((0, 0), (0, GATE_COLS - w_gate.shape[1]))).astype(BF16)
    p64, gate = _inproj(x, ln1.reshape(1, D_MODEL), w_main, w_gate)

    v128 = jnp.transpose(p64[OFF_AV:OFF_AV + 2 * DIFF_HEADS].reshape(DIFF_HEADS, 2, S, HEAD_DIM),
                         (0, 2, 1, 3)).reshape(DIFF_HEADS, S, 2 * HEAD_DIM)
    lam_init = 0.8 - 0.6 * math.exp(-0.3 * l)
    cst = jnp.zeros((1, 128), F32).at[0, 0].set(lam_init)
    o_a = _diff_attention(p64, v128, bias_a, diff_lambda.astype(F32), diff_subln.reshape(1, -1), cst)

    bf_row = jnp.pad(fox_bf.astype(F32), (0, GATE_COLS - FOX_HEADS)).reshape(1, GATE_COLS)
    fox_qa, fox_ka = _fox_cumsum(gate, bf_row, p64)
    o_b = _fox_attention(fox_qa, fox_ka, p64)

    half = CMP_STRIDE * HEAD_DIM
    chunks = p64[OFF_CKC:OFF_CKC + 2 * NSA_GROUPS].reshape(2 * NSA_GROUPS, S // CMP_STRIDE, half)
    pos = jnp.stack([cmp_pos_k, cmp_pos_v]).astype(F32).reshape(2, 2, half)
    w1 = jnp.stack([cmp_w1_k, cmp_w1_v]).astype(BF16)
    w2 = jnp.stack([cmp_w2_k, cmp_w2_v]).astype(BF16)
    kvc = _compress(chunks, pos, w1, w2)
    o_c, q_aug, k_aug = _cmp_attention(p64, kvc, overlap)
    o_s = _sel_attention(q_aug, k_aug, p64, bias_c[0])
    o_w = _win_attention(p64, bias_c[1])
    return o_a, o_b, o_c, o_s, o_w, gate


def _layer(x, l, final, ln1, w_in, diff_lambda, diff_subln, fox_bf, cmp_pos_k, cmp_w1_k, cmp_w2_k,
           cmp_pos_v, cmp_w1_v, cmp_w2_v, bias_a, bias_c, overlap, w_out, ln2, wg, bg, we, be,
           w_gate, w_up, w_down, ln_f):
    S = x.shape[0]
    o_a, o_b, o_c, o_s, o_w, gate = _attention_block(
        x, l, ln1, w_in, diff_lambda, diff_subln, fox_bf, cmp_pos_k, cmp_w1_k, cmp_w2_k,
        cmp_pos_v, cmp_w1_v, cmp_w2_v, bias_a, bias_c, overlap)
    wr = jnp.pad(jnp.concatenate([wg, we], axis=1).astype(F32), ((0, 0), (0, 128 - N_GROUPS - N_EXPERTS)))
    br = jnp.pad(jnp.concatenate([bg, be]).astype(F32), (0, 128 - N_GROUPS - N_EXPERTS)).reshape(1, 128)
    x1, h2, ri, rw, counts = _outproj(o_a, o_b, o_c, o_s, o_w, gate, x, w_out.astype(BF16),
                                      ln2.reshape(1, D_MODEL), wr, br)
    dest, chunk_e, n_used, n_chunks = _moe_plan(ri, counts)
    xbuf = _dispatch(dest, h2, jnp.zeros((n_chunks * MOE_CHUNK, D_MODEL), F32))
    ybuf = _experts(chunk_e + l * N_EXPERTS, n_used, xbuf, w_gate, w_up, w_down)
    return _combine(dest, x1, rw, ln_f.reshape(1, D_MODEL), ybuf, final)


def _make_biases(t5_table, S):
    t5_a, t5_c = t5_table[:, :DIFF_HEADS], t5_table[:, DIFF_HEADS:]
    c_t = min(C_T, S)
    return (_bias_tiles(t5_a, min(A_TQ, S), min(A_TK, S)),
            (_bias_tiles(t5_c, min(SEL_TQ, S), min(SEL_TK, S)), _win_band_bias(_bias_tiles(t5_c, c_t, c_t))))


def kernel(x, ln1, w_in, diff_lambda, diff_subln, fox_bf, cmp_pos_k, cmp_w1_k, cmp_w2_k, cmp_pos_v, cmp_w1_v,
           cmp_w2_v, t5_table, w_out, ln2, router_group_w, router_group_b, router_expert_w, router_expert_b,
           w_gate, w_up, w_down, ln_f):
    B, S, _ = x.shape
    assert B == 1
    depth = w_in.shape[0]
    bias_a, bias_c = _make_biases(t5_table, S)
    overlap = _overlap_matrix(S // CMP_STRIDE)
    xs = x.reshape(S, D_MODEL)
    wg_all = w_gate.reshape(depth * N_EXPERTS, D_MODEL, D_EXPERT)
    wu_all = w_up.reshape(depth * N_EXPERTS, D_MODEL, D_EXPERT)
    wd_all = w_down.reshape(depth * N_EXPERTS, D_EXPERT, D_MODEL)
    for l in range(depth):
        xs = _layer(xs, l, l == depth - 1, ln1[l], w_in[l], diff_lambda[l], diff_subln[l], fox_bf[l],
                    cmp_pos_k[l], cmp_w1_k[l], cmp_w2_k[l], cmp_pos_v[l], cmp_w1_v[l], cmp_w2_v[l],
                    bias_a, bias_c, overlap, w_out[l], ln2[l], router_group_w[l], router_group_b[l],
                    router_expert_w[l], router_expert_b[l], wg_all, wu_all, wd_all, ln_f)
    return xs.reshape(B, S, D_MODEL)
```

```python
import functools
import math

import numpy as np
import jax
import jax.numpy as jnp
from jax import lax
from jax.experimental import pallas as pl
from jax.experimental.pallas import tpu as pltpu

F32 = jnp.float32
BF16 = jnp.bfloat16
I32 = jnp.int32

D_MODEL = 2048
HEAD_DIM = 64
DIFF_HEADS = 4
FOX_HEADS = 8
NSA_HEADS = 16
NSA_GROUPS = 4
NSA_HPG = 4
CMP_BLOCK = 32
CMP_STRIDE = 16
CMP_HIDDEN = 256
SEL_BLOCK = 64
SEL_TOPK = 16
WINDOW = 512
NUM_BUCKETS = 32
MAX_DISTANCE = 128
N_GROUPS = 8
EXPERTS_PER_GROUP = 8
N_EXPERTS = 64
D_EXPERT = 256
MOE_CHUNK = 256
RMS_EPS = 1e-6
NEG_INF = -1e30
SEL_FORCE = 1e4
SCALE = HEAD_DIM ** -0.5

N_MAIN = 5632
B_FOX_F = 3072
B_NSA_G = 5640
GATE_COLS = 128
SEL_LANES = 128
SEL_MASKED = -32768.0
PICKED = -3.0e38

OFF_AQ, OFF_AK, OFF_AV = 0, 8, 16
OFF_BQ, OFF_BK, OFF_BV = 24, 32, 40
OFF_CQ = 48
OFF_CKC, OFF_CVC, OFF_CKS, OFF_CVS, OFF_CKW, OFF_CVW = 64, 68, 72, 76, 80, 84

VMEM_LIMIT = 56 * 1024 * 1024


def _cparams(n_axes):
    return pltpu.CompilerParams(dimension_semantics=("arbitrary",) * n_axes,
                                vmem_limit_bytes=VMEM_LIMIT)


IN_TM = 1024
IN_TN = 512


def _inproj_kernel(x_ref, g_ref, w_ref, wg_ref, o64_ref, og_ref, h_ref):
    j = pl.program_id(1)

    @pl.when(j == 0)
    def _():
        x = x_ref[...]
        ms = jnp.mean(x * x, axis=-1, keepdims=True)
        h_ref[...] = (x * lax.rsqrt(ms + RMS_EPS) * g_ref[...]).astype(BF16)

    res = jnp.dot(h_ref[...], w_ref[...], preferred_element_type=F32)
    for b in range(IN_TN // HEAD_DIM):
        o64_ref[b] = res[:, b * HEAD_DIM:(b + 1) * HEAD_DIM].astype(BF16)

    @pl.when(j == pl.num_programs(1) - 1)
    def _():
        og_ref[...] = jnp.dot(h_ref[...], wg_ref[...], preferred_element_type=F32)


def _inproj(x, g, w_main, w_gate):
    S = x.shape[0]
    tm = min(IN_TM, S)
    nj = N_MAIN // IN_TN
    return pl.pallas_call(
        _inproj_kernel,
        grid=(S // tm, nj),
        in_specs=[
            pl.BlockSpec((tm, D_MODEL), lambda i, j: (i, 0)),
            pl.BlockSpec((1, D_MODEL), lambda i, j: (0, 0)),
            pl.BlockSpec((D_MODEL, IN_TN), lambda i, j: (0, j)),
            pl.BlockSpec((D_MODEL, GATE_COLS), lambda i, j: (0, 0)),
        ],
        out_specs=[
            pl.BlockSpec((IN_TN // HEAD_DIM, tm, HEAD_DIM), lambda i, j: (j, i, 0)),
            pl.BlockSpec((tm, GATE_COLS), lambda i, j: (i, 0)),
        ],
        out_shape=[
            jax.ShapeDtypeStruct((N_MAIN // HEAD_DIM, S, HEAD_DIM), BF16),
            jax.ShapeDtypeStruct((S, GATE_COLS), F32),
        ],
        scratch_shapes=[pltpu.VMEM((tm, D_MODEL), BF16)],
        compiler_params=_cparams(2),
        name="inproj",
    )(x, g, w_main, w_gate)


CUM_TB = 512


def _split3(x):
    hi = x.astype(BF16)
    r1 = x - hi.astype(F32)
    mid = r1.astype(BF16)
    lo = (r1 - mid.astype(F32)).astype(BF16)
    return hi, mid, lo


def _cumsum_kernel(gate_ref, bf_ref, q_ref, k_ref, qa_ref, ka_ref, carry_ref):
    i = pl.program_id(0)

    @pl.when(i == 0)
    def _():
        carry_ref[...] = jnp.zeros_like(carry_ref)

    z = gate_ref[...] + bf_ref[...]
    log_f = -(jnp.maximum(-z, 0.0) + jnp.log1p(jnp.exp(-jnp.abs(z))))
    tb = log_f.shape[0]
    row = lax.broadcasted_iota(I32, (tb, tb), 0)
    col = lax.broadcasted_iota(I32, (tb, tb), 1)
    tri = jnp.where(row >= col, 1.0, 0.0).astype(BF16)
    hi, mid, lo = _split3(log_f)
    c = (jnp.dot(tri, hi, preferred_element_type=F32)
         + jnp.dot(tri, mid, preferred_element_type=F32)
         + jnp.dot(tri, lo, preferred_element_type=F32)) + carry_ref[...]
    carry_ref[...] = c[tb - 1:tb, :]
    c2 = c * LOG2E
    lane = lax.broadcasted_iota(I32, (tb, HEAD_DIM), 1)
    for h in range(FOX_HEADS):
        parts = [t.astype(F32) for t in _split3(c2[:, h:h + 1])]
        tail_q = jnp.where((lane >= 3) & (lane < 6), 1.0, 0.0)
        tail_k = jnp.where(lane < 3, 1.0, 0.0)
        for j, t in enumerate(parts):
            tail_q = jnp.where(lane == j, t, tail_q)
            tail_k = jnp.where(lane == 3 + j, -t, tail_k)
        qa_ref[h, :, 0:HEAD_DIM] = q_ref[h] * (SCALE * LOG2E)
        qa_ref[h, :, HEAD_DIM:] = tail_q.astype(BF16)
        ka_ref[h, :, 0:HEAD_DIM] = k_ref[h]
        ka_ref[h, :, HEAD_DIM:] = tail_k.astype(BF16)


def _fox_cumsum(gate, bf_row, p64):
    S = gate.shape[0]
    tb = min(CUM_TB, S)
    assert OFF_BQ % FOX_HEADS == 0 and OFF_BK % FOX_HEADS == 0
    out = jax.ShapeDtypeStruct((FOX_HEADS, S, 2 * HEAD_DIM), BF16)
    return pl.pallas_call(
        _cumsum_kernel,
        grid=(S // tb,),
        in_specs=[pl.BlockSpec((tb, GATE_COLS), lambda i: (i, 0)),
                  pl.BlockSpec((1, GATE_COLS), lambda i: (0, 0)),
                  pl.BlockSpec((FOX_HEADS, tb, HEAD_DIM), lambda i: (OFF_BQ // FOX_HEADS, i, 0)),
                  pl.BlockSpec((FOX_HEADS, tb, HEAD_DIM), lambda i: (OFF_BK // FOX_HEADS, i, 0))],
        out_specs=[pl.BlockSpec((FOX_HEADS, tb, 2 * HEAD_DIM), lambda i: (0, i, 0)),
                   pl.BlockSpec((FOX_HEADS, tb, 2 * HEAD_DIM), lambda i: (0, i, 0))],
        out_shape=[out, out],
        scratch_shapes=[pltpu.VMEM((1, GATE_COLS), F32)],
        compiler_params=_cparams(1),
        name="fox_cumsum",
    )(gate, bf_row, p64, p64)


def _tile_pairs(S, tq, tk):
    assert tk % tq == 0
    qi_l, ki_l, cls_l, last_l = [], [], [], []
    for qi in range(S // tq):
        n_k = (qi * tq + tq - 1) // tk + 1
        for ki in range(n_k):
            qi_l.append(qi)
            ki_l.append(ki)
            cls_l.append((qi * tq - ki * tk) // tq)
            last_l.append(int(ki == n_k - 1))
    return tuple(jnp.asarray(np.asarray(a, np.int32)) for a in (qi_l, ki_l, cls_l, last_l))


def _n_bias_classes(tq, tk):
    return -(-(tk + MAX_DISTANCE) // tq)


STAT_LANES = 128
LOG2E = 1.4426950408889634


def _chunk_rows(tk):
    return 32 if tk <= 512 else 16


def _flash_chains(chains, s_ref, p_ref, alpha_ref, m_ref, l_ref, acc_ref):
    width = acc_ref.shape[1]

    def park(chain):
        scores_fn, _, row0, nrows, _ = chain
        s_ref[row0:row0 + nrows, :] = scores_fn()

    def softmax_pv(chain):
        _, v, row0, nrows, addend = chain
        tk = s_ref.shape[1]
        reps = tk // STAT_LANES
        rc = _chunk_rows(tk)

        def chunk(c):
            sl = slice(row0 + c * rc, row0 + (c + 1) * rc)
            s = s_ref[sl, :]
            return sl, (s if addend is None else s + addend(c * rc, rc))

        for c in range(nrows // rc):
            sl, s = chunk(c)
            m_prev = m_ref[sl, :]
            m_new = jnp.maximum(m_prev, jnp.max(s, axis=1, keepdims=True))
            alpha_ref[sl, :] = jnp.exp2(m_prev - m_new)
            m_ref[sl, :] = m_new
        for c in range(nrows // rc):
            sl, s = chunk(c)
            p = jnp.exp2(s - jnp.tile(m_ref[sl, :], (1, reps)))
            part = p[:, 0:STAT_LANES]
            for r in range(1, reps):
                part = part + p[:, r * STAT_LANES:(r + 1) * STAT_LANES]
            l_ref[sl, :] = alpha_ref[sl, :] * l_ref[sl, :] + part
            p_ref[sl, :] = p.astype(BF16)
        rows = slice(row0, row0 + nrows)
        acc_ref[rows, :] = (alpha_ref[rows, 0:width] * acc_ref[rows, :]
                            + jnp.dot(p_ref[rows, :], v, preferred_element_type=F32))

    park(chains[0])
    for i, chain in enumerate(chains):
        if i + 1 < len(chains):
            park(chains[i + 1])
        softmax_pv(chain)


def _row_sum(l):
    return jnp.broadcast_to(jnp.sum(l, axis=1, keepdims=True), l.shape)


def _init_state(m_ref, l_ref, acc_ref):
    m_ref[...] = jnp.full_like(m_ref, NEG_INF)
    l_ref[...] = jnp.zeros_like(l_ref)
    acc_ref[...] = jnp.zeros_like(acc_ref)


def _qk(q, k):
    return lax.dot_general(q, k, (((1,), (1,)), ((), ())), preferred_element_type=F32)


def _t5_bucket(rel):
    n = jnp.maximum(rel, 0)
    max_exact = NUM_BUCKETS // 2
    nf = jnp.maximum(n, max_exact).astype(F32)
    large = max_exact + (jnp.log(nf / max_exact) / math.log(MAX_DISTANCE / max_exact)
                         * (NUM_BUCKETS - max_exact)).astype(I32)
    return jnp.where(n < max_exact, n, jnp.minimum(large, NUM_BUCKETS - 1))


def _bias_tiles(table, tq, tk):
    n_cls = _n_bias_classes(tq, tk)
    i = jnp.arange(tq)[:, None]
    j = jnp.arange(tk)[None, :]
    rel = jnp.stack([c * tq + i - j for c in range(n_cls)])
    t = table.astype(F32)
    bucket = _t5_bucket(rel)[..., None]
    b = jnp.zeros(rel.shape + (t.shape[1],), F32)
    for k in range(NUM_BUCKETS - 1):
        b = jnp.where(bucket == k, (t[k] - t[NUM_BUCKETS - 1]) * LOG2E, b)
    b = jnp.where((rel >= 0)[..., None], b, NEG_INF)
    return jnp.transpose(b, (3, 0, 1, 2))


A_TQ = 1024
A_TK = 1024
A_SPLIT = 4


def _diff_kernel(n_cls, qt_ref, kt_ref, cls_ref, last_ref, q1_ref, q2_ref, k1_ref, k2_ref, v_ref, bias_ref,
                 dl_ref, sub_ref, cst_ref, o_ref, qs, s_ref, p_ref, alpha, m, l, acc):
    p = pl.program_id(1)
    ki = kt_ref[p]
    T = q1_ref.shape[1]
    half = T // A_SPLIT

    @pl.when(ki == 0)
    def _():
        _init_state(m, l, acc)
        qs[0:T, :] = q1_ref[0] * (SCALE * LOG2E)
        qs[T:2 * T, :] = q2_ref[0] * (SCALE * LOG2E)

    def step(with_bias):
        chains = []
        for j, k_ref in enumerate((k1_ref, k2_ref)):
            for hh in range(A_SPLIT):
                r0 = j * T + hh * half
                bias = ((lambda r, n, hh=hh: bias_ref[0, 0, hh * half + r:hh * half + r + n, :])
                        if with_bias else None)
                chains.append((lambda r0=r0, k_ref=k_ref: _qk(qs[r0:r0 + half, :], k_ref[0]),
                               v_ref[0], r0, half, bias))
        _flash_chains(chains, s_ref, p_ref, alpha, m, l, acc)

    @pl.when(cls_ref[p] < n_cls)
    def _():
        step(True)

    @pl.when(cls_ref[p] >= n_cls)
    def _():
        step(False)

    @pl.when(last_ref[p] == 1)
    def _():
        dl = dl_ref[...]
        lam_init = cst_ref[0:1, 0:1]
        lam = (jnp.exp(jnp.sum(dl[0:1] * dl[1:2], axis=1, keepdims=True))
               - jnp.exp(jnp.sum(dl[2:3] * dl[3:4], axis=1, keepdims=True)) + lam_init)
        o = acc[0:T, :] / _row_sum(l[0:T, :]) - lam * (acc[T:2 * T, :] / _row_sum(l[T:2 * T, :]))
        ms = jnp.mean(o * o, axis=-1, keepdims=True)
        y = o * lax.rsqrt(ms + RMS_EPS) * sub_ref[...]
        o_ref[0] = (y * (1.0 - lam_init)).astype(BF16)


def _diff_attention(p64, v128, bias, dl, subln, cst):
    S = p64.shape[1]
    T, TK = bias.shape[2], bias.shape[3]
    n_cls = bias.shape[1]
    tabs = _tile_pairs(S, T, TK)
    qblk = lambda off: pl.BlockSpec((1, T, HEAD_DIM), lambda h, p, qt, kt, cl, la: (off + 2 * h, qt[p], 0))
    kblk = lambda off: pl.BlockSpec((1, TK, HEAD_DIM), lambda h, p, qt, kt, cl, la: (off + 2 * h, kt[p], 0))
    const = lambda shape: pl.BlockSpec(shape, lambda h, p, qt, kt, cl, la: (0,) * len(shape))
    grid_spec = pltpu.PrefetchScalarGridSpec(
        num_scalar_prefetch=4,
        grid=(DIFF_HEADS, int(tabs[0].shape[0])),
        in_specs=[
            qblk(OFF_AQ), qblk(OFF_AQ + 1), kblk(OFF_AK), kblk(OFF_AK + 1),
            pl.BlockSpec((1, TK, 2 * HEAD_DIM), lambda h, p, qt, kt, cl, la: (h, kt[p], 0)),
            pl.BlockSpec((1, 1, T, TK), lambda h, p, qt, kt, cl, la: (h, jnp.minimum(cl[p], n_cls - 1), 0, 0)),
            const((4, HEAD_DIM)), const((1, 2 * HEAD_DIM)), const((1, 128)),
        ],
        out_specs=pl.BlockSpec((1, T, 2 * HEAD_DIM), lambda h, p, qt, kt, cl, la: (h, qt[p], 0)),
        scratch_shapes=[
            pltpu.VMEM((2 * T, HEAD_DIM), BF16),
            pltpu.VMEM((2 * T, TK), F32), pltpu.VMEM((2 * T, TK), BF16), pltpu.VMEM((2 * T, STAT_LANES), F32),
            pltpu.VMEM((2 * T, STAT_LANES), F32), pltpu.VMEM((2 * T, STAT_LANES), F32),
            pltpu.VMEM((2 * T, 2 * HEAD_DIM), F32),
        ],
    )
    return pl.pallas_call(
        functools.partial(_diff_kernel, n_cls),
        grid_spec=grid_spec,
        out_shape=jax.ShapeDtypeStruct((DIFF_HEADS, S, 2 * HEAD_DIM), BF16),
        compiler_params=_cparams(2),
        name="diff_attn",
    )(*tabs, p64, p64, p64, p64, v128, bias, dl, subln, cst)


B_TQ = 1024
B_TK = 1024
B_CHAINS = 4


def _fox_kernel(diag_cls, qt_ref, kt_ref, cls_ref, last_ref, q_ref, k_ref, v_ref, o_ref,
                s_ref, p_ref, alpha, m, l, acc):
    p = pl.program_id(1)
    T = q_ref.shape[1]
    TK = k_ref.shape[1]
    half = T // B_CHAINS

    @pl.when(kt_ref[p] == 0)
    def _():
        _init_state(m, l, acc)

    def step(diag):
        lead = cls_ref[p] * T
        chains = []
        for hh in range(B_CHAINS):
            r0 = hh * half

            def causal(r, n, r0=r0):
                row = lead + (r0 + r) + lax.broadcasted_iota(I32, (n, TK), 0)
                col = lax.broadcasted_iota(I32, (n, TK), 1)
                return jnp.where(row >= col, 0.0, NEG_INF)

            chains.append((lambda r0=r0: _qk(q_ref[0, r0:r0 + half, :], k_ref[0]), v_ref[0], r0, half,
                           causal if diag else None))
        _flash_chains(chains, s_ref, p_ref, alpha, m, l, acc)

    @pl.when(cls_ref[p] < diag_cls)
    def _():
        step(True)

    @pl.when(cls_ref[p] >= diag_cls)
    def _():
        step(False)

    @pl.when(last_ref[p] == 1)
    def _():
        o_ref[0] = (acc[...] / _row_sum(l[...])[:, 0:HEAD_DIM]).astype(BF16)


def _fox_attention(qa, ka, p64):
    S = p64.shape[1]
    T = min(B_TQ, S)
    TK = min(B_TK, S)
    aug = qa.shape[2]
    tabs = _tile_pairs(S, T, TK)
    grid_spec = pltpu.PrefetchScalarGridSpec(
        num_scalar_prefetch=4,
        grid=(FOX_HEADS, int(tabs[0].shape[0])),
        in_specs=[
            pl.BlockSpec((1, T, aug), lambda h, p, qt, kt, cl, la: (h, qt[p], 0)),
            pl.BlockSpec((1, TK, aug), lambda h, p, qt, kt, cl, la: (h, kt[p], 0)),
            pl.BlockSpec((1, TK, HEAD_DIM), lambda h, p, qt, kt, cl, la: (OFF_BV + h, kt[p], 0)),
        ],
        out_specs=pl.BlockSpec((1, T, HEAD_DIM), lambda h, p, qt, kt, cl, la: (h, qt[p], 0)),
        scratch_shapes=[
            pltpu.VMEM((T, TK), F32), pltpu.VMEM((T, TK), BF16), pltpu.VMEM((T, STAT_LANES), F32),
            pltpu.VMEM((T, STAT_LANES), F32), pltpu.VMEM((T, STAT_LANES), F32), pltpu.VMEM((T, HEAD_DIM), F32),
        ],
    )
    return pl.pallas_call(
        functools.partial(_fox_kernel, TK // T),
        grid_spec=grid_spec,
        out_shape=jax.ShapeDtypeStruct((FOX_HEADS, S, HEAD_DIM), BF16),
        compiler_params=_cparams(2),
        name="fox_attn",
    )(*tabs, qa, ka, p64)


def _gelu_tanh(x):
    return 0.5 * x * (1.0 + jnp.tanh(math.sqrt(2.0 / math.pi) * (x + 0.044715 * (x * x * x))))


def _compress_kernel(x_ref, pos_ref, w1_ref, w2_ref, o_ref):
    x = x_ref[0].astype(F32)
    half = x.shape[1]
    pos = pos_ref[0]
    x_lo = (x + pos[0:1]).astype(BF16)
    x_hi = (x + pos[1:2]).astype(BF16)
    w1 = w1_ref[0]
    y_lo = jnp.dot(x_lo, w1[:half], preferred_element_type=F32)
    y_hi = jnp.dot(x_hi, w1[half:], preferred_element_type=F32)
    n = y_hi.shape[0]
    pre = y_lo + pltpu.roll(y_hi, n - 1, 0)
    o_ref[0] = jnp.dot(_gelu_tanh(pre).astype(BF16), w2_ref[0], preferred_element_type=F32).astype(BF16)


def _compress(chunks, pos, w1, w2):
    n = chunks.shape[1]
    half = CMP_STRIDE * HEAD_DIM
    return pl.pallas_call(
        _compress_kernel,
        grid=(2 * NSA_GROUPS,),
        in_specs=[
            pl.BlockSpec((1, n, half), lambda i: (i, 0, 0)),
            pl.BlockSpec((1, 2, half), lambda i: (i // NSA_GROUPS, 0, 0)),
            pl.BlockSpec((1, 2 * half, CMP_HIDDEN), lambda i: (i // NSA_GROUPS, 0, 0)),
            pl.BlockSpec((1, CMP_HIDDEN, HEAD_DIM), lambda i: (i // NSA_GROUPS, 0, 0)),
        ],
        out_specs=pl.BlockSpec((1, n, HEAD_DIM), lambda i: (i, 0, 0)),
        out_shape=jax.ShapeDtypeStruct((2 * NSA_GROUPS, n, HEAD_DIM), BF16),
        compiler_params=_cparams(1),
        name="nsa_compress",
    )(chunks, pos, w1, w2)


C_T = 256


def _cmp_kernel(q_ref, kc_ref, vc_ref, ks_ref, ov_ref, oc_ref, qa_ref, ka_ref):
    qi = pl.program_id(1)
    T = q_ref.shape[1]
    ncp = kc_ref.shape[1]
    q0 = qi * T
    qs = (q_ref[...] * SCALE).reshape(NSA_HPG * T, HEAD_DIM)
    s = _qk(qs, kc_ref[0]).reshape(NSA_HPG, T, ncp)
    t = q0 + lax.broadcasted_iota(I32, (T, ncp), 0)
    n_idx = lax.broadcasted_iota(I32, (T, ncp), 1)
    visible = (CMP_STRIDE * n_idx + (CMP_BLOCK - 1) <= t)[None]
    s = jnp.where(visible, s, NEG_INF)
    smax = jnp.max(s, axis=2, keepdims=True)
    e = jnp.where(visible, jnp.exp(s - smax), 0.0)
    den = jnp.sum(e, axis=2, keepdims=True)
    p = e / jnp.where(den > 0.0, den, 1.0)
    o = jnp.dot(p.reshape(NSA_HPG * T, ncp).astype(BF16), vc_ref[0], preferred_element_type=F32)
    oc_ref[...] = o.reshape(NSA_HPG, T, HEAD_DIM).astype(BF16)

    psum = p[0] + p[1] + p[2] + p[3]
    hi, mid, lo = _split3(psum)
    ov = ov_ref[...]
    imp = (jnp.dot(hi, ov, preferred_element_type=F32) + jnp.dot(mid, ov, preferred_element_type=F32)
           + jnp.dot(lo, ov, preferred_element_type=F32))
    tq = q0 + lax.broadcasted_iota(I32, (T, SEL_LANES), 0)
    jb = lax.broadcasted_iota(I32, (T, SEL_LANES), 1)
    cur = tq // SEL_BLOCK
    forced = (jb == 0) | (jb == cur) | (jb == cur - 1)
    valid = SEL_BLOCK * jb <= tq
    score = jnp.where(valid, imp + jnp.where(forced, SEL_FORCE, 0.0), NEG_INF)
    sc = score.T
    jrow = lax.broadcasted_iota(I32, sc.shape, 0).astype(F32)
    sel = jnp.zeros(sc.shape, F32)
    for _ in range(SEL_TOPK):
        mx = jnp.max(sc, axis=0, keepdims=True)
        first = jnp.min(jnp.where(sc == mx, jrow, float(SEL_LANES)), axis=0, keepdims=True)
        pick = jrow == first
        sel = jnp.where(pick, 1.0, sel)
        sc = jnp.where(pick, PICKED, sc)
    sel_neg = jnp.where(sel.T > 0.5, 0.0, SEL_MASKED).astype(BF16)

    zeros = jnp.zeros((T, HEAD_DIM), BF16)
    for hh in range(NSA_HPG):
        qa_ref[hh, :, 0:HEAD_DIM] = q_ref[hh] * (SCALE * LOG2E)
        qa_ref[hh, :, HEAD_DIM:2 * HEAD_DIM] = zeros
        qa_ref[hh, :, 2 * HEAD_DIM:] = sel_neg
    kblk = (q0 + lax.broadcasted_iota(I32, (T, SEL_LANES), 0)) // SEL_BLOCK
    onehot = jnp.where(kblk == jb, 1.0, 0.0).astype(BF16)
    ka_ref[0, :, 0:HEAD_DIM] = ks_ref[0]
    ka_ref[0, :, HEAD_DIM:2 * HEAD_DIM] = zeros
    ka_ref[0, :, 2 * HEAD_DIM:] = onehot


def _cmp_attention(p64, kvc, overlap):
    S = p64.shape[1]
    T = min(C_T, S)
    ncp = kvc.shape[1]
    aug = 2 * HEAD_DIM + SEL_LANES
    return pl.pallas_call(
        _cmp_kernel,
        grid=(NSA_GROUPS, S // T),
        in_specs=[
            pl.BlockSpec((NSA_HPG, T, HEAD_DIM), lambda g, i: (OFF_CQ // NSA_HPG + g, i, 0)),
            pl.BlockSpec((1, ncp, HEAD_DIM), lambda g, i: (g, 0, 0)),
            pl.BlockSpec((1, ncp, HEAD_DIM), lambda g, i: (NSA_GROUPS + g, 0, 0)),
            pl.BlockSpec((1, T, HEAD_DIM), lambda g, i: (OFF_CKS + g, i, 0)),
            pl.BlockSpec((ncp, SEL_LANES), lambda g, i: (0, 0)),
        ],
        out_specs=[
            pl.BlockSpec((NSA_HPG, T, HEAD_DIM), lambda g, i: (g, i, 0)),
            pl.BlockSpec((NSA_HPG, T, aug), lambda g, i: (g, i, 0)),
            pl.BlockSpec((1, T, aug), lambda g, i: (g, i, 0)),
        ],
        out_shape=[
            jax.ShapeDtypeStruct((NSA_HEADS, S, HEAD_DIM), BF16),
            jax.ShapeDtypeStruct((NSA_HEADS, S, aug), BF16),
            jax.ShapeDtypeStruct((NSA_GROUPS, S, aug), BF16),
        ],
        compiler_params=_cparams(2),
        name="nsa_cmp_attn",
    )(p64, kvc, kvc, p64, overlap)


def _overlap_matrix(ncp):
    n = np.arange(ncp)[:, None] * CMP_STRIDE
    j = np.arange(SEL_LANES)[None, :] * SEL_BLOCK
    ov = np.clip(np.minimum(n + CMP_BLOCK, j + SEL_BLOCK) - np.maximum(n, j), 0, None) / CMP_BLOCK
    ov[ncp - 1:] = 0.0
    return jnp.asarray(ov, BF16)


SEL_TQ = 512
SEL_TK = 1024
SEL_SPLIT = 2


def _sel_kernel(n_cls, qt_ref, kt_ref, cls_ref, last_ref, qa_ref, ka_ref, v_ref, bias_ref, o_ref,
                s_ref, p_ref, alpha, m, l, acc):
    p = pl.program_id(1)
    T = qa_ref.shape[1]

    @pl.when(kt_ref[p] == 0)
    def _():
        _init_state(m, l, acc)

    def step(with_bias):
        part = T // SEL_SPLIT
        chains = []
        for h in range(NSA_HPG):
            for hh in range(SEL_SPLIT):
                lo = hh * part
                bias = (lambda r, n, h=h, lo=lo: bias_ref[h, 0, lo + r:lo + r + n, :]) if with_bias else None
                chains.append((lambda h=h, lo=lo: _qk(qa_ref[h, lo:lo + part, :], ka_ref[0]),
                               v_ref[0], h * T + lo, part, bias))
        _flash_chains(chains, s_ref, p_ref, alpha, m, l, acc)

    @pl.when(cls_ref[p] < n_cls)
    def _():
        step(True)

    @pl.when(cls_ref[p] >= n_cls)
    def _():
        step(False)

    @pl.when(last_ref[p] == 1)
    def _():
        o_ref[...] = (acc[...] / _row_sum(l[...])[:, :HEAD_DIM]).reshape(NSA_HPG, T, HEAD_DIM).astype(BF16)


def _sel_attention(q_aug, k_aug, p64, bias):
    S = p64.shape[1]
    n_cls, T, TK = bias.shape[1:]
    aug = q_aug.shape[2]
    tabs = _tile_pairs(S, T, TK)
    grid_spec = pltpu.PrefetchScalarGridSpec(
        num_scalar_prefetch=4,
        grid=(NSA_GROUPS, int(tabs[0].shape[0])),
        in_specs=[
            pl.BlockSpec((NSA_HPG, T, aug), lambda g, p, qt, kt, cl, la: (g, qt[p], 0)),
            pl.BlockSpec((1, TK, aug), lambda g, p, qt, kt, cl, la: (g, kt[p], 0)),
            pl.BlockSpec((1, TK, HEAD_DIM), lambda g, p, qt, kt, cl, la: (OFF_CVS + g, kt[p], 0)),
            pl.BlockSpec((NSA_HPG, 1, T, TK), lambda g, p, qt, kt, cl, la: (g, jnp.minimum(cl[p], n_cls - 1), 0, 0)),
        ],
        out_specs=pl.BlockSpec((NSA_HPG, T, HEAD_DIM), lambda g, p, qt, kt, cl, la: (g, qt[p], 0)),
        scratch_shapes=[pltpu.VMEM((NSA_HPG * T, TK), F32), pltpu.VMEM((NSA_HPG * T, TK), BF16),
                        pltpu.VMEM((NSA_HPG * T, STAT_LANES), F32),
                        pltpu.VMEM((NSA_HPG * T, STAT_LANES), F32), pltpu.VMEM((NSA_HPG * T, STAT_LANES), F32),
                        pltpu.VMEM((NSA_HPG * T, HEAD_DIM), F32)],
    )
    return pl.pallas_call(
        functools.partial(_sel_kernel, n_cls),
        grid_spec=grid_spec,
        out_shape=jax.ShapeDtypeStruct((NSA_HEADS, S, HEAD_DIM), BF16),
        compiler_params=_cparams(2),
        name="nsa_sel_attn",
    )(*tabs, q_aug, k_aug, p64, bias)


def _win_kernel(q_ref, k2_ref, k1_ref, k0_ref, v2_ref, v1_ref, v0_ref, bias_ref, o_ref,
                s_ref, p_ref, alpha, m, l, acc):
    qi = pl.program_id(1)
    T = q_ref.shape[1]
    _init_state(m, l, acc)
    v = jnp.concatenate([v2_ref[0], v1_ref[0], v0_ref[0]], axis=0)
    first_col = (2 - jnp.minimum(qi, 2)) * T

    def step(edge):
        chains = []
        for h in range(NSA_HPG):
            def bias(r, n, h=h):
                b = bias_ref[h, r:r + n, :]
                if edge:
                    col = lax.broadcasted_iota(I32, b.shape, 1)
                    b = jnp.where(col >= first_col, b, NEG_INF)
                return b

            def scores(h=h):
                q = q_ref[h] * (SCALE * LOG2E)
                return jnp.concatenate([_qk(q, k2_ref[0]), _qk(q, k1_ref[0]), _qk(q, k0_ref[0])], axis=1)

            chains.append((scores, v, h * T, T, bias))
        _flash_chains(chains, s_ref, p_ref, alpha, m, l, acc)

    @pl.when(qi >= 2)
    def _():
        step(False)

    @pl.when(qi < 2)
    def _():
        step(True)

    o_ref[...] = (acc[...] / _row_sum(l[...])[:, 0:HEAD_DIM]).reshape(NSA_HPG, T, HEAD_DIM).astype(BF16)


def _win_band_bias(bias):
    T = bias.shape[2]
    i = jnp.arange(T)[:, None]
    j = jnp.arange(T)[None, :]
    far = jnp.broadcast_to(jnp.where(j > i, 0.0, NEG_INF).astype(F32), (bias.shape[0], T, T))
    return jnp.concatenate([far, bias[:, 1], bias[:, 0]], axis=-1)


def _win_attention(p64, bias):
    S = p64.shape[1]
    T = bias.shape[1]
    assert WINDOW == 2 * T and bias.shape[2] == 3 * T
    kv = lambda off, back: pl.BlockSpec((1, T, HEAD_DIM), lambda g, i: (off + g, jnp.maximum(i - back, 0), 0))
    return pl.pallas_call(
        _win_kernel,
        grid=(NSA_GROUPS, S // T),
        in_specs=[
            pl.BlockSpec((NSA_HPG, T, HEAD_DIM), lambda g, i: (OFF_CQ // NSA_HPG + g, i, 0)),
            kv(OFF_CKW, 2), kv(OFF_CKW, 1), kv(OFF_CKW, 0), kv(OFF_CVW, 2), kv(OFF_CVW, 1), kv(OFF_CVW, 0),
            pl.BlockSpec((NSA_HPG, T, 3 * T), lambda g, i: (g, 0, 0)),
        ],
        out_specs=pl.BlockSpec((NSA_HPG, T, HEAD_DIM), lambda g, i: (g, i, 0)),
        out_shape=jax.ShapeDtypeStruct((NSA_HEADS, S, HEAD_DIM), BF16),
        scratch_shapes=[pltpu.VMEM((NSA_HPG * T, 3 * T), F32), pltpu.VMEM((NSA_HPG * T, 3 * T), BF16),
                        pltpu.VMEM((NSA_HPG * T, STAT_LANES), F32),
                        pltpu.VMEM((NSA_HPG * T, STAT_LANES), F32), pltpu.VMEM((NSA_HPG * T, STAT_LANES), F32),
                        pltpu.VMEM((NSA_HPG * T, HEAD_DIM), F32)],
        compiler_params=_cparams(2),
        name="nsa_win_attn",
    )(p64, p64, p64, p64, p64, p64, p64, bias)


OUT_TM = 256
RT_E1, RT_E2, RT_R1, RT_R2 = 0, 1, 2, 3
ROUTER_LANE0 = N_GROUPS


def _outproj_kernel(oa_ref, ob_ref, oc_ref, os_ref, ow_ref, gate_ref, x_ref, w_ref, ln_ref, wr_ref, br_ref,
                    x1_ref, h2_ref, ri_ref, rw_ref, cnt_ref, mix_ref, carry_ref):
    i = pl.program_id(0)
    tm = x_ref.shape[0]

    @pl.when(i == 0)
    def _():
        carry_ref[...] = jnp.zeros_like(carry_ref)

    for h in range(DIFF_HEADS):
        mix_ref[:, 128 * h:128 * (h + 1)] = oa_ref[h]
    for h in range(FOX_HEADS):
        c0 = 512 + HEAD_DIM * h
        mix_ref[:, c0:c0 + HEAD_DIM] = ob_ref[h]
    sig = jax.nn.sigmoid(gate_ref[...])
    for h in range(NSA_HEADS):
        c0 = 1024 + HEAD_DIM * h
        g0 = sig[:, FOX_HEADS + h:FOX_HEADS + h + 1]
        g1 = sig[:, FOX_HEADS + NSA_HEADS + h:FOX_HEADS + NSA_HEADS + h + 1]
        g2 = sig[:, FOX_HEADS + 2 * NSA_HEADS + h:FOX_HEADS + 2 * NSA_HEADS + h + 1]
        o = g0 * oc_ref[h].astype(F32) + g1 * os_ref[h].astype(F32) + g2 * ow_ref[h].astype(F32)
        mix_ref[:, c0:c0 + HEAD_DIM] = o.astype(BF16)

    x1 = x_ref[...] + jnp.dot(mix_ref[...], w_ref[...], preferred_element_type=F32)
    x1_ref[...] = x1
    ms = jnp.mean(x1 * x1, axis=-1, keepdims=True)
    h2 = x1 * lax.rsqrt(ms + RMS_EPS) * ln_ref[...]
    h2_ref[...] = h2

    a0, a1, _ = _split3(h2)
    b0, b1, _ = _split3(wr_ref[...])
    dot = lambda a, b: jnp.dot(a, b, preferred_element_type=F32)
    logits = dot(a0, b0) + (dot(a0, b1) + dot(a1, b0)) + br_ref[...]
    lane = lax.broadcasted_iota(I32, logits.shape, 1)
    lane_f = lane.astype(F32)

    def first_lane(cond):
        return jnp.min(jnp.where(cond, lane_f, 128.0), axis=1, keepdims=True).astype(I32)

    is_g = lane < N_GROUPS
    lg = jnp.where(is_g, logits, NEG_INF)
    gmax = jnp.max(lg, axis=1, keepdims=True)
    gsum = jnp.sum(jnp.where(is_g, jnp.exp(lg - gmax), 0.0), axis=1, keepdims=True)
    g_p = 1.0 / gsum
    g_idx = first_lane(lg == gmax)
    e_lane = lane - ROUTER_LANE0
    in_group = (e_lane >= 0) & (e_lane < N_EXPERTS) & ((e_lane >> 3) == g_idx)
    em = jnp.where(in_group, logits, NEG_INF)
    v1 = jnp.max(em, axis=1, keepdims=True)
    i1 = first_lane(em == v1)
    em2 = jnp.where(lane == i1, PICKED, em)
    v2 = jnp.max(em2, axis=1, keepdims=True)
    i2 = first_lane(em2 == v2)
    tt = jnp.exp(v2 - v1)
    w1 = g_p / (1.0 + tt)
    w2 = g_p * tt / (1.0 + tt)

    oh = jnp.where((lane == i1) | (lane == i2), 1.0, 0.0)
    row = lax.broadcasted_iota(I32, (tm, tm), 0)
    col = lax.broadcasted_iota(I32, (tm, tm), 1)
    tri = jnp.where(row > col, 1.0, 0.0).astype(BF16)
    prefix = jnp.dot(tri, oh.astype(BF16), preferred_element_type=F32) + carry_ref[...]
    r1 = jnp.sum(jnp.where(lane == i1, prefix, 0.0), axis=1, keepdims=True)
    r2 = jnp.sum(jnp.where(lane == i2, prefix, 0.0), axis=1, keepdims=True)
    carry = carry_ref[...] + jnp.sum(oh, axis=0, keepdims=True)
    carry_ref[...] = carry
    cnt_ref[...] = carry

    ri = jnp.where(lane == RT_E1, i1 - ROUTER_LANE0,
                   jnp.where(lane == RT_E2, i2 - ROUTER_LANE0,
                             jnp.where(lane == RT_R1, r1.astype(I32),
                                       jnp.where(lane == RT_R2, r2.astype(I32), 0))))
    ri_ref[...] = ri
    rw_ref[...] = jnp.where(lane == 0, w1, jnp.where(lane == 1, w2, 0.0))


def _outproj(o_a, o_b, o_c, o_s, o_w, gate, x, w_out, ln2, wr, br):
    S = x.shape[0]
    tm = min(OUT_TM, S)
    full = lambda shape: pl.BlockSpec(shape, lambda i: (0,) * len(shape))
    return pl.pallas_call(
        _outproj_kernel,
        grid=(S // tm,),
        in_specs=[
            pl.BlockSpec((DIFF_HEADS, tm, 2 * HEAD_DIM), lambda i: (0, i, 0)),
            pl.BlockSpec((FOX_HEADS, tm, HEAD_DIM), lambda i: (0, i, 0)),
            pl.BlockSpec((NSA_HEADS, tm, HEAD_DIM), lambda i: (0, i, 0)),
            pl.BlockSpec((NSA_HEADS, tm, HEAD_DIM), lambda i: (0, i, 0)),
            pl.BlockSpec((NSA_HEADS, tm, HEAD_DIM), lambda i: (0, i, 0)),
            pl.BlockSpec((tm, GATE_COLS), lambda i: (i, 0)),
            pl.BlockSpec((tm, D_MODEL), lambda i: (i, 0)),
            full((D_MODEL, D_MODEL)),
            full((1, D_MODEL)),
            full((D_MODEL, 128)),
            full((1, 128)),
        ],
        out_specs=[
            pl.BlockSpec((tm, D_MODEL), lambda i: (i, 0)),
            pl.BlockSpec((tm, D_MODEL), lambda i: (i, 0)),
            pl.BlockSpec((tm, 128), lambda i: (i, 0)),
            pl.BlockSpec((tm, 128), lambda i: (i, 0)),
            pl.BlockSpec((1, 128), lambda i: (0, 0)),
        ],
        out_shape=[
            jax.ShapeDtypeStruct((S, D_MODEL), F32),
            jax.ShapeDtypeStruct((S, D_MODEL), F32),
            jax.ShapeDtypeStruct((S, 128), I32),
            jax.ShapeDtypeStruct((S, 128), F32),
            jax.ShapeDtypeStruct((1, 128), F32),
        ],
        scratch_shapes=[pltpu.VMEM((tm, D_MODEL), BF16), pltpu.VMEM((1, 128), F32)],
        compiler_params=_cparams(1),
        name="outproj_router",
    )(o_a, o_b, o_c, o_s, o_w, gate, x, w_out, ln2, wr, br)


DISP_TM = 256
ROW_UNROLL = 8


def _dispatch_kernel(dest_ref, h_ref, xb_in_ref, xb_ref, sem):
    del xb_in_ref
    i = pl.program_id(0)
    base = i * DISP_TM

    def copy(t, k):
        return pltpu.make_async_copy(h_ref.at[pl.ds(t, 1)],
                                     xb_ref.at[pl.ds(dest_ref[2 * (base + t) + k], 1)], sem)

    def start(g, c):
        for u in range(ROW_UNROLL):
            copy(g * ROW_UNROLL + u, 0).start()
            copy(g * ROW_UNROLL + u, 1).start()
        return c

    def wait(g, c):
        for _ in range(2 * ROW_UNROLL):
            pltpu.make_async_copy(h_ref.at[pl.ds(0, 1)], xb_ref.at[pl.ds(0, 1)], sem).wait()
        return c

    lax.fori_loop(0, DISP_TM // ROW_UNROLL, start, 0)
    lax.fori_loop(0, DISP_TM // ROW_UNROLL, wait, 0)


def _dispatch(dest_flat, h2, xbuf0):
    S = h2.shape[0]
    assert S % DISP_TM == 0
    grid_spec = pltpu.PrefetchScalarGridSpec(
        num_scalar_prefetch=1,
        grid=(S // DISP_TM,),
        in_specs=[pl.BlockSpec((DISP_TM, D_MODEL), lambda i, d: (i, 0)), pl.BlockSpec(memory_space=pl.ANY)],
        out_specs=pl.BlockSpec(memory_space=pl.ANY),
        scratch_shapes=[pltpu.SemaphoreType.DMA(())],
    )
    return pl.pallas_call(
        _dispatch_kernel,
        grid_spec=grid_spec,
        out_shape=jax.ShapeDtypeStruct(xbuf0.shape, xbuf0.dtype),
        input_output_aliases={2: 0},
        compiler_params=pltpu.CompilerParams(dimension_semantics=("arbitrary",), has_side_effects=True),
        name="moe_dispatch",
    )(dest_flat, h2, xbuf0)


def _expert_kernel(ce_ref, used_ref, x_ref, wg_ref, wu_ref, wd_ref, y_ref):
    c = pl.program_id(0)

    @pl.when(c < used_ref[0])
    def _():
        x = x_ref[...].astype(BF16)
        g = jnp.dot(x, wg_ref[0].astype(BF16), preferred_element_type=F32)
        u = jnp.dot(x, wu_ref[0].astype(BF16), preferred_element_type=F32)
        hmid = (g * jax.nn.sigmoid(g) * u).astype(BF16)
        y_ref[...] = jnp.dot(hmid, wd_ref[0].astype(BF16), preferred_element_type=F32)

    @pl.when(c >= used_ref[0])
    def _():
        y_ref[...] = jnp.zeros_like(y_ref)


def _experts(chunk_e, n_used, xbuf, w_gate, w_up, w_down):
    P = xbuf.shape[0]
    n_chunks = P // MOE_CHUNK
    row_blk = lambda c, ce, nu: (jnp.minimum(c, nu[0] - 1), 0)
    wt_blk = lambda c, ce, nu: (ce[jnp.minimum(c, nu[0] - 1)], 0, 0)
    grid_spec = pltpu.PrefetchScalarGridSpec(
        num_scalar_prefetch=2,
        grid=(n_chunks,),
        in_specs=[
            pl.BlockSpec((MOE_CHUNK, D_MODEL), row_blk),
            pl.BlockSpec((1, D_MODEL, D_EXPERT), wt_blk),
            pl.BlockSpec((1, D_MODEL, D_EXPERT), wt_blk),
            pl.BlockSpec((1, D_EXPERT, D_MODEL), wt_blk),
        ],
        out_specs=pl.BlockSpec((MOE_CHUNK, D_MODEL), lambda c, ce, nu: (c, 0)),
    )
    return pl.pallas_call(
        _expert_kernel,
        grid_spec=grid_spec,
        out_shape=jax.ShapeDtypeStruct((P, D_MODEL), F32),
        compiler_params=_cparams(1),
        name="moe_experts",
    )(chunk_e, n_used, xbuf, w_gate, w_up, w_down)


COMB_TM = 256


def _combine_kernel(final, dest_ref, x_ref, rw_ref, ln_ref, y_ref, o_ref, r0, r1, sem):
    i = pl.program_id(0)
    base = i * COMB_TM

    def copy(t, k):
        dst = (r0, r1)[k]
        return pltpu.make_async_copy(y_ref.at[pl.ds(dest_ref[2 * (base + t) + k], 1)], dst.at[pl.ds(t, 1)], sem)

    def start(g, c):
        for u in range(ROW_UNROLL):
            copy(g * ROW_UNROLL + u, 0).start()
            copy(g * ROW_UNROLL + u, 1).start()
        return c

    def wait(g, c):
        for _ in range(2 * ROW_UNROLL):
            pltpu.make_async_copy(y_ref.at[pl.ds(0, 1)], r0.at[pl.ds(0, 1)], sem).wait()
        return c

    lax.fori_loop(0, COMB_TM // ROW_UNROLL, start, 0)
    lax.fori_loop(0, COMB_TM // ROW_UNROLL, wait, 0)
    rw = rw_ref[...]
    x2 = x_ref[...] + (rw[:, 0:1] * r0[...] + rw[:, 1:2] * r1[...])
    if final:
        ms = jnp.mean(x2 * x2, axis=-1, keepdims=True)
        x2 = x2 * lax.rsqrt(ms + RMS_EPS) * ln_ref[...]
    o_ref[...] = x2


def _combine(dest_flat, x1, rw, ln_f, ybuf, final):
    S = x1.shape[0]
    assert S % COMB_TM == 0
    grid_spec = pltpu.PrefetchScalarGridSpec(
        num_scalar_prefetch=1,
        grid=(S // COMB_TM,),
        in_specs=[
            pl.BlockSpec((COMB_TM, D_MODEL), lambda i, d: (i, 0)),
            pl.BlockSpec((COMB_TM, 128), lambda i, d: (i, 0)),
            pl.BlockSpec((1, D_MODEL), lambda i, d: (0, 0)),
            pl.BlockSpec(memory_space=pl.ANY),
        ],
        out_specs=pl.BlockSpec((COMB_TM, D_MODEL), lambda i, d: (i, 0)),
        scratch_shapes=[pltpu.VMEM((COMB_TM, D_MODEL), F32), pltpu.VMEM((COMB_TM, D_MODEL), F32),
                        pltpu.SemaphoreType.DMA(())],
    )
    return pl.pallas_call(
        functools.partial(_combine_kernel, final),
        grid_spec=grid_spec,
        out_shape=jax.ShapeDtypeStruct((S, D_MODEL), F32),
        compiler_params=_cparams(1),
        name="moe_combine",
    )(dest_flat, x1, rw, ln_f, ybuf)


def _moe_plan(ri, counts):
    S = ri.shape[0]
    cnt = counts[0, ROUTER_LANE0:ROUTER_LANE0 + N_EXPERTS].astype(I32)
    padded = (cnt + MOE_CHUNK - 1) // MOE_CHUNK * MOE_CHUNK
    pend = jnp.cumsum(padded)
    off = pend - padded
    eid = ri[:, RT_E1:RT_E2 + 1]
    rank = ri[:, RT_R1:RT_R2 + 1]
    dest = (off[eid] + rank).reshape(-1)
    n_chunks = (S * 2) // MOE_CHUNK + N_EXPERTS
    chunk_e = jnp.minimum(jnp.searchsorted(pend, jnp.arange(n_chunks, dtype=I32) * MOE_CHUNK, side='right'),
                          N_EXPERTS - 1).astype(I32)
    n_used = (pend[-1:] // MOE_CHUNK).astype(I32)
    return dest, chunk_e, n_used, n_chunks


def _attention_block(x, l, ln1, w_in, diff_lambda, diff_subln, fox_bf, cmp_pos_k, cmp_w1_k, cmp_w2_k,
                     cmp_pos_v, cmp_w1_v, cmp_w2_v, bias_a, bias_c, overlap):
    S = x.shape[0]
    w_main = jnp.concatenate([w_in[:, :B_FOX_F], w_in[:, B_FOX_F + FOX_HEADS:B_NSA_G]], axis=1).astype(BF16)
    w_gate = jnp.concatenate([w_in[:, B_FOX_F:B_FOX_F + FOX_HEADS], w_in[:, B_NSA_G:]], axis=1)
    w_gate = jnp.pad(w_gate, ((0, 0), (0, GATE_COLS - w_gate.shape[1]))).astype(BF16)
    p64, gate = _inproj(x, ln1.reshape(1, D_MODEL), w_main, w_gate)

    v128 = jnp.transpose(p64[OFF_AV:OFF_AV + 2 * DIFF_HEADS].reshape(DIFF_HEADS, 2, S, HEAD_DIM),
                         (0, 2, 1, 3)).reshape(DIFF_HEADS, S, 2 * HEAD_DIM)
    lam_init = 0.8 - 0.6 * math.exp(-0.3 * l)
    cst = jnp.zeros((1, 128), F32).at[0, 0].set(lam_init)
    o_a = _diff_attention(p64, v128, bias_a, diff_lambda.astype(F32), diff_subln.reshape(1, -1), cst)

    bf_row = jnp.pad(fox_bf.astype(F32), (0, GATE_COLS - FOX_HEADS)).reshape(1, GATE_COLS)
    fox_qa, fox_ka = _fox_cumsum(gate, bf_row, p64)
    o_b = _fox_attention(fox_qa, fox_ka, p64)

    half = CMP_STRIDE * HEAD_DIM
    chunks = p64[OFF_CKC:OFF_CKC + 2 * NSA_GROUPS].reshape(2 * NSA_GROUPS, S // CMP_STRIDE, half)
    pos = jnp.stack([cmp_pos_k, cmp_pos_v]).astype(F32).reshape(2, 2, half)
    w1 = jnp.stack([cmp_w1_k, cmp_w1_v]).astype(BF16)
    w2 = jnp.stack([cmp_w2_k, cmp_w2_v]).astype(BF16)
    kvc = _compress(chunks, pos, w1, w2)
    o_c, q_aug, k_aug = _cmp_attention(p64, kvc, overlap)
    o_s = _sel_attention(q_aug, k_aug, p64, bias_c[0])
    o_w = _win_attention(p64, bias_c[1])
    return o_a, o_b, o_c, o_s, o_w, gate


def _layer(x, l, final, ln1, w_in, diff_lambda, diff_subln, fox_bf, cmp_pos_k, cmp_w1_k, cmp_w2_k,
           cmp_pos_v, cmp_w1_v, cmp_w2_v, bias_a, bias_c, overlap, w_out, ln2, wg, bg, we, be,
           w_gate, w_up, w_down, ln_f):
    S = x.shape[0]
    o_a, o_b, o_c, o_s, o_w, gate = _attention_block(
        x, l, ln1, w_in, diff_lambda, diff_subln, fox_bf, cmp_pos_k, cmp_w1_k, cmp_w2_k,
        cmp_pos_v, cmp_w1_v, cmp_w2_v, bias_a, bias_c, overlap)
    wr = jnp.pad(jnp.concatenate([wg, we], axis=1).astype(F32), ((0, 0), (0, 128 - N_GROUPS - N_EXPERTS)))
    br = jnp.pad(jnp.concatenate([bg, be]).astype(F32), (0, 128 - N_GROUPS - N_EXPERTS)).reshape(1, 128)
    x1, h2, ri, rw, counts = _outproj(o_a, o_b, o_c, o_s, o_w, gate, x, w_out.astype(BF16),
                                      ln2.reshape(1, D_MODEL), wr, br)
    dest, chunk_e, n_used, n_chunks = _moe_plan(ri, counts)
    xbuf = _dispatch(dest, h2, jnp.zeros((n_chunks * MOE_CHUNK, D_MODEL), F32))
    ybuf = _experts(chunk_e + l * N_EXPERTS, n_used, xbuf, w_gate, w_up, w_down)
    return _combine(dest, x1, rw, ln_f.reshape(1, D_MODEL), ybuf, final)


def _make_biases(t5_table, S):
    t5_a, t5_c = t5_table[:, :DIFF_HEADS], t5_table[:, DIFF_HEADS:]
    c_t = min(C_T, S)
    return (_bias_tiles(t5_a, min(A_TQ, S), min(A_TK, S)),
            (_bias_tiles(t5_c, min(SEL_TQ, S), min(SEL_TK, S)), _win_band_bias(_bias_tiles(t5_c, c_t, c_t))))


def kernel(x, ln1, w_in, diff_lambda, diff_subln, fox_bf, cmp_pos_k, cmp_w1_k, cmp_w2_k, cmp_pos_v, cmp_w1_v,
           cmp_w2_v, t5_table, w_out, ln2, router_group_w, router_group_b, router_expert_w, router_expert_b,
           w_gate, w_up, w_down, ln_f):
    B, S, _ = x.shape
    assert B == 1
    depth = w_in.shape[0]
    bias_a, bias_c = _make_biases(t5_table, S)
    overlap = _overlap_matrix(S // CMP_STRIDE)
    xs = x.reshape(S, D_MODEL)
    wg_all = w_gate.reshape(depth * N_EXPERTS, D_MODEL, D_EXPERT)
    wu_all = w_up.reshape(depth * N_EXPERTS, D_MODEL, D_EXPERT)
    wd_all = w_down.reshape(depth * N_EXPERTS, D_EXPERT, D_MODEL)
    for l in range(depth):
        xs = _layer(xs, l, l == depth - 1, ln1[l], w_in[l], diff_lambda[l], diff_subln[l], fox_bf[l],
                    cmp_pos_k[l], cmp_w1_k[l], cmp_w2_k[l], cmp_pos_v[l], cmp_w1_v[l], cmp_w2_v[l],
                    bias_a, bias_c, overlap, w_out[l], ln2[l], router_group_w[l], router_group_b[l],
                    router_expert_w[l], router_expert_b[l], wg_all, wu_all, wd_all, ln_f)
    return xs.reshape(B, S, D_MODEL)
```
